```python
import math
import jax, jax.numpy as jnp
from jax import lax
import numpy as np

D_MODEL = 1024
BATCH = 8
SEQ = 2048
DEPTH = 4

SSM_WIDTH = 512
SSM_GROUP = 16
SSM_GROUPS = SSM_WIDTH // SSM_GROUP
SSM_STATE = 64
DT_MIN = 1e-3
DT_MAX = 1e-1
HEAD_DIM = 128
HEADS_PER_GROUP = 4
DILATION_PATTERNS = ((128, 1), (512, 4), (2048, 16))
N_ATTN_GROUPS = len(DILATION_PATTERNS)
ATTN_HEADS = N_ATTN_GROUPS * HEADS_PER_GROUP
ATTN_QKV_WIDTH = ATTN_HEADS * HEAD_DIM
ATTN_WIDTH = HEADS_PER_GROUP * HEAD_DIM
IN_SPLITS = (SSM_WIDTH, SSM_WIDTH, ATTN_QKV_WIDTH, ATTN_QKV_WIDTH, ATTN_QKV_WIDTH,
             ATTN_WIDTH, D_MODEL, D_MODEL)
IN_COLS = sum(IN_SPLITS)
SPLIT_POINTS = tuple(int(s) for s in np.cumsum(IN_SPLITS)[:-1])
RMS_EPS = 1e-6

kernel_name = "hybrid_s5_dilated_attn_gated_block"


def rmsnorm(x, g):
    xf = x.astype(jnp.float32)
    inv = lax.rsqrt(jnp.mean(xf * xf, axis=-1, keepdims=True) + RMS_EPS)
    return (xf * inv * g.astype(jnp.float32)).astype(x.dtype)


def _ssm_combine(left, right):
    a_l, b_l = left
    a_r, b_r = right
    return a_r * a_l, a_r * b_l + b_r


def s5_branch(u, lam_re, lam_im, log_dt, b_re, b_im, c_re, c_im, d_skip, w_glu, b_glu):
    bsz, L, _ = u.shape
    uf = u.astype(jnp.float32)
    ug = uf.reshape(bsz, L, SSM_GROUPS, SSM_GROUP)
    lam = lax.complex(jnp.minimum(lam_re.astype(jnp.float32), -1e-4), lam_im.astype(jnp.float32))
    dt = jnp.exp(log_dt.astype(jnp.float32))[:, None]
    lam_bar = jnp.exp(lam * dt)
    b = lax.complex(b_re.astype(jnp.float32), b_im.astype(jnp.float32))
    b_bar = ((lam_bar - 1.0) / lam)[..., None] * b
    c = lax.complex(c_re.astype(jnp.float32), c_im.astype(jnp.float32))
    drive = jnp.einsum('gpc,blgc->blgp', b_bar, ug)
    decay = jnp.broadcast_to(lam_bar, drive.shape)
    _, states = lax.associative_scan(_ssm_combine, (decay, drive), axis=1)
    y = jnp.einsum('gcp,blgp->blgc', c, states).real.reshape(bsz, L, SSM_WIDTH)
    y = y + d_skip.astype(jnp.float32) * uf
    y = jax.nn.gelu(y)
    y = y * jax.nn.sigmoid(y @ w_glu.astype(jnp.float32) + b_glu.astype(jnp.float32))
    return y.astype(u.dtype)


def dilated_group_attention(q, k, v, window, dilation):
    bsz, L, hg, hd = q.shape
    span = window // dilation
    n = L // dilation
    nb = -(-n // span)
    pad = nb * span - n

    def to_sub(t):
        return t.reshape(bsz, n, dilation, hg, hd).transpose(0, 2, 3, 1, 4)

    qb = jnp.pad(to_sub(q), ((0, 0),) * 3 + ((0, pad), (0, 0))).reshape(bsz, dilation, hg, nb, span, hd)

    def kv_blocks(t):
        tp = jnp.pad(to_sub(t), ((0, 0),) * 3 + ((span, pad), (0, 0))).reshape(bsz, dilation, hg, nb + 1, span, hd)
        return jnp.concatenate([tp[:, :, :, :-1], tp[:, :, :, 1:]], axis=4)

    kb, vb = kv_blocks(k), kv_blocks(v)
    scores = jnp.einsum('brhnqd,brhnkd->brhnqk', qb, kb).astype(jnp.float32) * (hd ** -0.5)
    qi = jnp.arange(span)[:, None]
    ki = jnp.arange(2 * span)[None, :]
    blk = jnp.arange(nb)[:, None, None]
    dist = span + qi - ki
    valid = (dist >= 0) & (dist <= span) & (blk * span + ki - span >= 0)
    scores = jnp.where(valid, scores, -jnp.inf)
    lse = jax.nn.logsumexp(scores, axis=-1)
    p = jnp.exp(scores - lse[..., None])
    out = jnp.einsum('brhnqk,brhnkd->brhnqd', p.astype(v.dtype), vb)

    def from_sub(t):
        rest = t.shape[5:]
        t = t.reshape(bsz, dilation, hg, nb * span, *rest)[:, :, :, :n]
        t = jnp.moveaxis(t, 3, 1)
        return t.reshape(bsz, L, hg, *rest)

    return from_sub(out), from_sub(lse)


def dilated_attention_branch(q, k, v):
    bsz, L, _ = q.shape
    shp = (bsz, L, N_ATTN_GROUPS, HEADS_PER_GROUP, HEAD_DIM)
    q, k, v = q.reshape(shp), k.reshape(shp), v.reshape(shp)
    outs, lses = [], []
    for gi, (window, dilation) in enumerate(DILATION_PATTERNS):
        o, s = dilated_group_attention(q[:, :, gi], k[:, :, gi], v[:, :, gi], window, dilation)
        outs.append(o)
        lses.append(s)
    outs = jnp.stack(outs, axis=0)
    alpha = jax.nn.softmax(jnp.stack(lses, axis=0), axis=0)
    y = jnp.sum(alpha[..., None] * outs.astype(jnp.float32), axis=0)
    return y.reshape(bsz, L, ATTN_WIDTH).astype(q.dtype)


def setup_inputs(seed: int = 0) -> dict:
    key = jax.random.key(seed)
    ks = jax.random.split(key, 20)
    f32 = jnp.float32
    nrm = lambda k, shape, s: jax.random.normal(k, shape, f32) * s
    x = jax.random.normal(ks[0], (BATCH, SEQ, D_MODEL), f32)
    pre_norm_g = 1.0 + nrm(ks[1], (DEPTH, D_MODEL), 0.02)
    w_in = nrm(ks[2], (DEPTH, D_MODEL, IN_COLS), D_MODEL ** -0.5)
    lambda_re = -0.5 + nrm(ks[3], (DEPTH, SSM_GROUPS, SSM_STATE), 0.01)
    lambda_im = (math.pi * jnp.arange(SSM_STATE, dtype=f32))[None, None, :] + nrm(ks[4], (DEPTH, SSM_GROUPS, SSM_STATE), 0.01)
    log_dt = jax.random.uniform(ks[5], (DEPTH, SSM_GROUPS), f32, math.log(DT_MIN), math.log(DT_MAX))
    b_scale = (2.0 * SSM_GROUP) ** -0.5
    b_re = nrm(ks[6], (DEPTH, SSM_GROUPS, SSM_STATE, SSM_GROUP), b_scale)
    b_im = nrm(ks[7], (DEPTH, SSM_GROUPS, SSM_STATE, SSM_GROUP), b_scale)
    c_scale = (2.0 * SSM_STATE) ** -0.5
    c_re = nrm(ks[8], (DEPTH, SSM_GROUPS, SSM_GROUP, SSM_STATE), c_scale)
    c_im = nrm(ks[9], (DEPTH, SSM_GROUPS, SSM_GROUP, SSM_STATE), c_scale)
    d_skip = nrm(ks[10], (DEPTH, SSM_WIDTH), 1.0)
    w_glu = nrm(ks[11], (DEPTH, SSM_WIDTH, SSM_WIDTH), SSM_WIDTH ** -0.5)
    b_glu = nrm(ks[12], (DEPTH, SSM_WIDTH), 0.01)
    w_branch_s = nrm(ks[13], (DEPTH, SSM_WIDTH, D_MODEL), SSM_WIDTH ** -0.5)
    w_branch_a = nrm(ks[14], (DEPTH, ATTN_WIDTH, D_MODEL), ATTN_WIDTH ** -0.5)
    w_out = nrm(ks[15], (DEPTH, D_MODEL, D_MODEL), D_MODEL ** -0.5)
    post_norm_g = 1.0 + nrm(ks[16], (DEPTH, D_MODEL), 0.02)
    return {"x": x, "pre_norm_g": pre_norm_g, "w_in": w_in, "lambda_re": lambda_re,
            "lambda_im": lambda_im, "log_dt": log_dt, "b_re": b_re, "b_im": b_im,
            "c_re": c_re, "c_im": c_im, "d_skip": d_skip, "w_glu": w_glu, "b_glu": b_glu,
            "w_branch_s": w_branch_s, "w_branch_a": w_branch_a, "w_out": w_out,
            "post_norm_g": post_norm_g}


def reference(x, pre_norm_g, w_in, lambda_re, lambda_im, log_dt, b_re, b_im, c_re, c_im,
              d_skip, w_glu, b_glu, w_branch_s, w_branch_a, w_out, post_norm_g):
    for l in range(DEPTH):
        h = rmsnorm(x, pre_norm_g[l])
        proj = h @ w_in[l]
        u_s, z_s, q, k, v, z_a, g_s, g_a = jnp.split(proj, SPLIT_POINTS, axis=-1)
        y_s = s5_branch(u_s, lambda_re[l], lambda_im[l], log_dt[l], b_re[l], b_im[l],
                        c_re[l], c_im[l], d_skip[l], w_glu[l], b_glu[l]) * jax.nn.silu(z_s)
        y_a = dilated_attention_branch(q, k, v) * jax.nn.silu(z_a)
        merged = (jax.nn.sigmoid(g_s) * (y_s @ w_branch_s[l])
                  + jax.nn.sigmoid(g_a) * (y_a @ w_branch_a[l]))
        out = merged @ w_out[l]
        x = x + rmsnorm(out, post_norm_g[l]).astype(x.dtype)
    return x
```

```python
import functools
import math

import jax
import jax.numpy as jnp
from jax import lax
from jax.experimental import pallas as pl
from jax.experimental.pallas import tpu as pltpu

F32 = jnp.float32
BF16 = jnp.bfloat16

RMS_EPS = 1e-6
SSM_WIDTH = 512
SSM_GROUP = 16
SSM_GROUPS = 32
SSM_STATE = 64
N_STATE = SSM_GROUPS * SSM_STATE
HEAD_DIM = 128
HEADS_PER_GROUP = 4
SPAN = 128
DILATIONS = (1, 4, 16)
ATTN_WIDTH = HEADS_PER_GROUP * HEAD_DIM
LANES = 128
VMEM_LIMIT = 56 * 1024 * 1024

PROJ_TN = 512
N_QKV_TILES = 9
N_PROJ_TILES = 16


def _rmsnorm_f32(x, g):
    inv = lax.rsqrt(jnp.mean(x * x, axis=-1, keepdims=True) + RMS_EPS)
    return x * inv * g


def _proj_kernel(x_ref, g_ref, w_ref, o1_ref, o2_ref, o3_ref, pa_ref,
                 hf_ref, hn_ref, h4_ref, h16_ref, *, tm):
    j = pl.program_id(2)
    d_model = x_ref.shape[-1]
    n_slab = d_model // LANES

    @pl.when(j == 0)
    def _():
        hf = _rmsnorm_f32(x_ref[0], g_ref[...])
        hn_ref[...] = hf.astype(BF16)
        for k in range(n_slab):
            hf_ref[k] = hf[:, k * LANES:(k + 1) * LANES]
        for r, dst in ((4, h4_ref), (16, h16_ref)):
            rows = tm // r
            for c in range(r):
                for k in range(n_slab):
                    dst[c * rows:(c + 1) * rows, k * LANES:(k + 1) * LANES] = (
                        hf_ref[k, pl.ds(c, rows, stride=r), :].astype(BF16))

    w = w_ref[...]

    @pl.when(j < 3)
    def _():
        o1_ref[0, 0] = jnp.dot(hn_ref[...], w, preferred_element_type=F32).astype(BF16)

    @pl.when((j >= 3) & (j < 6))
    def _():
        o2_ref[0] = jnp.dot(h4_ref[...], w, preferred_element_type=F32).astype(BF16).reshape(
            4, tm // 4, PROJ_TN)

    @pl.when((j >= 6) & (j < N_QKV_TILES))
    def _():
        o3_ref[0] = jnp.dot(h16_ref[...], w, preferred_element_type=F32).astype(BF16).reshape(
            16, tm // 16, PROJ_TN)

    @pl.when(j >= N_QKV_TILES)
    def _():
        pa_ref[...] = jnp.dot(hn_ref[...], w, preferred_element_type=F32)


def _in_projection(x, g, w_perm, *, tm=512):
    bsz, seq, d_model = x.shape
    n_i = seq // tm
    kern = functools.partial(_proj_kernel, tm=tm)
    n_pa = N_PROJ_TILES - N_QKV_TILES
    return pl.pallas_call(
        kern,
        grid=(bsz, n_i, N_PROJ_TILES),
        in_specs=[
            pl.BlockSpec((1, tm, d_model), lambda b, i, j: (b, i, 0)),
            pl.BlockSpec((1, d_model), lambda b, i, j: (0, 0)),
            pl.BlockSpec((d_model, PROJ_TN), lambda b, i, j: (0, j)),
        ],
        out_specs=[
            pl.BlockSpec((1, 1, tm, PROJ_TN), lambda b, i, j: (b, 0, i, jnp.clip(j, 0, 2))),
            pl.BlockSpec((1, 4, tm // 4, PROJ_TN), lambda b, i, j: (b, 0, i, jnp.clip(j - 3, 0, 2))),
            pl.BlockSpec((1, 16, tm // 16, PROJ_TN), lambda b, i, j: (b, 0, i, jnp.clip(j - 6, 0, 2))),
            pl.BlockSpec((tm, PROJ_TN),
                         lambda b, i, j: (b * n_i + i, jnp.clip(j - N_QKV_TILES, 0, n_pa - 1))),
        ],
        out_shape=[
            jax.ShapeDtypeStruct((bsz, 1, seq, 3 * ATTN_WIDTH), BF16),
            jax.ShapeDtypeStruct((bsz, 4, seq // 4, 3 * ATTN_WIDTH), BF16),
            jax.ShapeDtypeStruct((bsz, 16, seq // 16, 3 * ATTN_WIDTH), BF16),
            jax.ShapeDtypeStruct((bsz * seq, n_pa * PROJ_TN), F32),
        ],
        scratch_shapes=[
            pltpu.VMEM((d_model // LANES, tm, LANES), F32),
            pltpu.VMEM((tm, d_model), BF16),
            pltpu.VMEM((tm, d_model), BF16),
            pltpu.VMEM((tm, d_model), BF16),
        ],
        compiler_params=pltpu.CompilerParams(
            dimension_semantics=("arbitrary", "arbitrary", "arbitrary"),
            vmem_limit_bytes=VMEM_LIMIT),
        name="in_projection",
    )(x, g, w_perm)


def _s5_kernel(uz_ref, bmat_ref, cmat_ref, lre_ref, lim_ref, dskip_ref, wglu_ref, bglu_ref,
               ys_ref, drive_ref, state_ref, *, tt, bsz, lane_chunk):
    @pl.when(pl.program_id(0) == 0)
    def _():
        state_ref[...] = jnp.zeros_like(state_ref)

    u = uz_ref[:, :SSM_WIDTH]
    z = uz_ref[:, SSM_WIDTH:]
    drive_ref[...] = jnp.dot(u.astype(BF16), bmat_ref[...], preferred_element_type=F32)

    for lo in range(0, N_STATE, lane_chunk):
        re_sl = slice(lo, lo + lane_chunk)
        im_sl = slice(N_STATE + lo, N_STATE + lo + lane_chunk)
        lre = jnp.broadcast_to(lre_ref[:, re_sl], (bsz, lane_chunk))
        lim = jnp.broadcast_to(lim_ref[:, re_sl], (bsz, lane_chunk))

        def step(t, carry, re_sl=re_sl, im_sl=im_sl, lre=lre, lim=lim):
            hre, him = carry
            rows = pl.ds(pl.multiple_of(t * bsz, bsz), bsz)
            nre = lre * hre - lim * him + drive_ref[rows, re_sl]
            nim = lre * him + lim * hre + drive_ref[rows, im_sl]
            drive_ref[rows, re_sl] = nre
            drive_ref[rows, im_sl] = nim
            return nre, nim

        hre, him = lax.fori_loop(0, tt, step, (state_ref[:, re_sl], state_ref[:, im_sl]))
        state_ref[:, re_sl] = hre
        state_ref[:, im_sl] = him

    y = jnp.dot(drive_ref[...].astype(BF16), cmat_ref[...], preferred_element_type=F32)
    y = y + dskip_ref[...] * u
    y = jax.nn.gelu(y)
    gate = jnp.dot(y.astype(BF16), wglu_ref[...], preferred_element_type=F32) + bglu_ref[...]
    y = y * jax.nn.sigmoid(gate)
    ys_ref[...] = y * (z * jax.nn.sigmoid(z))


def _s5_branch(uz_tm, bmat, cmat, lre, lim, d_skip, w_glu, b_glu, *, bsz, tt=64, lane_chunk=512):
    rows_total = uz_tm.shape[0]
    rows = tt * bsz
    kern = functools.partial(_s5_kernel, tt=tt, bsz=bsz, lane_chunk=lane_chunk)
    full = lambda shape: pl.BlockSpec(shape, lambda i: (0,) * len(shape))
    return pl.pallas_call(
        kern,
        grid=(rows_total // rows,),
        in_specs=[
            pl.BlockSpec((rows, 2 * SSM_WIDTH), lambda i: (i, 0)),
            full(bmat.shape), full(cmat.shape), full(lre.shape), full(lim.shape),
            full(d_skip.shape), full(w_glu.shape), full(b_glu.shape),
        ],
        out_specs=pl.BlockSpec((rows, SSM_WIDTH), lambda i: (i, 0)),
        out_shape=jax.ShapeDtypeStruct((rows_total, SSM_WIDTH), F32),
        scratch_shapes=[
            pltpu.VMEM((rows, 2 * N_STATE), F32),
            pltpu.VMEM((bsz, 2 * N_STATE), F32),
        ],
        compiler_params=pltpu.CompilerParams(
            dimension_semantics=("arbitrary",), vmem_limit_bytes=VMEM_LIMIT),
        name="s5_branch",
    )(uz_tm, bmat, cmat, lre, lim, d_skip, w_glu, b_glu)


def _attn_block(q, k, v, first):
    s = lax.dot_general(q, k, (((1,), (1,)), ((), ())), preferred_element_type=F32)
    s = s * (HEAD_DIM ** -0.5)
    qi = lax.broadcasted_iota(jnp.int32, s.shape, 0)
    ki = lax.broadcasted_iota(jnp.int32, s.shape, 1)
    if first:
        valid = ki <= qi
    else:
        valid = (ki >= qi) & (ki <= qi + SPAN)
    s = jnp.where(valid, s, -jnp.inf)
    m = jnp.max(s, axis=-1, keepdims=True)
    p = jnp.exp(s - m)
    l = jnp.sum(p, axis=-1, keepdims=True)
    acc = jnp.dot(p.astype(BF16), v, preferred_element_type=F32)
    return acc / l, m + jnp.log(l)


def _attn_kernel(q1_ref, k1_ref, v1_ref, q2_ref, k2_ref, v2_ref, q3_ref, k3_ref, v3_ref,
                 y_ref, out_ref, lse_ref):
    groups = ((q1_ref, k1_ref, v1_ref, 1), (q2_ref, k2_ref, v2_ref, 4), (q3_ref, k3_ref, v3_ref, 16))
    for gi, (q_ref, k_ref, v_ref, r) in enumerate(groups):
        n_blk = q_ref.shape[2] // SPAN

        def put(c, n, o, lse, gi=gi, r=r):
            if r == 1:
                rows = pl.ds(pl.multiple_of(n * SPAN, SPAN), SPAN)
            else:
                rows = pl.ds(c + r * SPAN * n, SPAN, stride=r)
            out_ref[gi, rows, :] = o
            lse_ref[gi, rows, :] = jnp.broadcast_to(lse, o.shape)

        for c in range(r):
            o, lse = _attn_block(q_ref[0, c, :SPAN, :], k_ref[0, c, :SPAN, :], v_ref[0, c, :SPAN, :],
                                 True)
            put(c, 0, o, lse)
            if n_blk > 1:
                def body(n, carry, c=c, q_ref=q_ref, k_ref=k_ref, v_ref=v_ref, put=put):
                    q_rows = pl.ds(pl.multiple_of(n * SPAN, SPAN), SPAN)
                    kv_rows = pl.ds(pl.multiple_of((n - 1) * SPAN, SPAN), 2 * SPAN)
                    o, lse = _attn_block(q_ref[0, c, q_rows, :], k_ref[0, c, kv_rows, :],
                                         v_ref[0, c, kv_rows, :], False)
                    put(c, n, o, lse)
                    return carry
                lax.fori_loop(1, n_blk, body, 0)

    lse = lse_ref[...]
    mx = jnp.max(lse, axis=0)
    w = jnp.exp(lse - mx[None])
    y_ref[0] = jnp.sum(w * out_ref[...], axis=0) / jnp.sum(w, axis=0)


def _attention(o1, o2, o3):
    bsz, _, seq, _ = o1.shape

    def specs(arr):
        _, r, n, _ = arr.shape
        return [pl.BlockSpec((1, r, n, HEAD_DIM), lambda b, h, part=part: (b, 0, 0, part * HEADS_PER_GROUP + h))
                for part in range(3)]

    return pl.pallas_call(
        _attn_kernel,
        grid=(bsz, HEADS_PER_GROUP),
        in_specs=specs(o1) + specs(o2) + specs(o3),
        out_specs=pl.BlockSpec((1, seq, HEAD_DIM), lambda b, h: (b, 0, h)),
        out_shape=jax.ShapeDtypeStruct((bsz, seq, ATTN_WIDTH), F32),
        scratch_shapes=[
            pltpu.VMEM((3, seq, HEAD_DIM), F32),
            pltpu.VMEM((3, seq, HEAD_DIM), F32),
        ],
        compiler_params=pltpu.CompilerParams(
            dimension_semantics=("arbitrary", "arbitrary"), vmem_limit_bytes=VMEM_LIMIT),
        name="dilated_attention",
    )(o1, o1, o1, o2, o2, o2, o3, o3, o3)


def _merge_kernel(x_ref, ys_ref, ya_ref, za_ref, gs_ref, ga_ref, wbs_ref, wba_ref, wout_ref, pg_ref,
                  o_ref):
    za = za_ref[...]
    ya = ya_ref[...] * (za * jax.nn.sigmoid(za))
    ms = jnp.dot(ys_ref[...].astype(BF16), wbs_ref[...], preferred_element_type=F32)
    ma = jnp.dot(ya.astype(BF16), wba_ref[...], preferred_element_type=F32)
    merged = jax.nn.sigmoid(gs_ref[...]) * ms + jax.nn.sigmoid(ga_ref[...]) * ma
    out = jnp.dot(merged.astype(BF16), wout_ref[...], preferred_element_type=F32)
    o_ref[...] = x_ref[...] + _rmsnorm_f32(out, pg_ref[...])


def _merge(x2d, ys, ya, pa, wbs, wba, wout, pg, *, tm=512):
    rows, d_model = x2d.shape
    full = lambda shape: pl.BlockSpec(shape, lambda i: (0,) * len(shape))
    return pl.pallas_call(
        _merge_kernel,
        grid=(rows // tm,),
        in_specs=[
            pl.BlockSpec((tm, d_model), lambda i: (i, 0)),
            pl.BlockSpec((tm, SSM_WIDTH), lambda i: (i, 0)),
            pl.BlockSpec((tm, ATTN_WIDTH), lambda i: (i, 0)),
            pl.BlockSpec((tm, ATTN_WIDTH), lambda i: (i, 6)),
            pl.BlockSpec((tm, d_model), lambda i: (i, 0)),
            pl.BlockSpec((tm, d_model), lambda i: (i, 1)),
            full(wbs.shape), full(wba.shape), full(wout.shape), full(pg.shape),
        ],
        out_specs=pl.BlockSpec((tm, d_model), lambda i: (i, 0)),
        out_shape=jax.ShapeDtypeStruct((rows, d_model), F32),
        compiler_params=pltpu.CompilerParams(
            dimension_semantics=("arbitrary",), vmem_limit_bytes=VMEM_LIMIT),
        name="merge_out",
    )(x2d, ys, ya, pa, pa, pa, wbs, wba, wout, pg)


def _permute_w_in(w):
    s, a = SSM_WIDTH, ATTN_WIDTH
    q0, k0, v0 = 2 * s, 2 * s + 3 * a, 2 * s + 6 * a
    za0 = 2 * s + 9 * a
    gs0 = za0 + a
    d = w.shape[0]
    ga0 = gs0 + d
    cols = []
    for gi in range(3):
        for base in (q0, k0, v0):
            cols.append(w[:, base + gi * a: base + (gi + 1) * a])
    cols += [w[:, gs0:gs0 + d], w[:, ga0:ga0 + d], w[:, 0:s], w[:, s:2 * s], w[:, za0:za0 + a]]
    return jnp.concatenate(cols, axis=1).astype(BF16)


def _ssm_params(lam_re, lam_im, log_dt, b_re, b_im, c_re, c_im):
    lam = lax.complex(jnp.minimum(lam_re, -1e-4), lam_im)
    dt = jnp.exp(log_dt)[:, None]
    lam_bar = jnp.exp(lam * dt)
    b_bar = ((lam_bar - 1.0) / lam)[..., None] * lax.complex(b_re, b_im)
    eye = jnp.eye(SSM_GROUPS, dtype=F32)
    def blk_in(m):
        return jnp.einsum('gpc,gh->gchp', m, eye).reshape(SSM_WIDTH, N_STATE)
    def blk_out(m):
        return jnp.einsum('gcp,gh->gphc', m, eye).reshape(N_STATE, SSM_WIDTH)
    bmat = jnp.concatenate([blk_in(b_bar.real), blk_in(b_bar.imag)], axis=1).astype(BF16)
    cmat = jnp.concatenate([blk_out(c_re), -blk_out(c_im)], axis=0).astype(BF16)
    lre = lam_bar.real.reshape(1, N_STATE)
    lim = lam_bar.imag.reshape(1, N_STATE)
    return bmat, cmat, lre, lim


def kernel(x, pre_norm_g, w_in, lambda_re, lambda_im, log_dt, b_re, b_im, c_re, c_im, d_skip, w_glu, b_glu, w_branch_s, w_branch_a, w_out, post_norm_g):
    bsz, seq, d_model = x.shape
    depth = w_in.shape[0]
    x2d = x.reshape(bsz * seq, d_model)
    for l in range(depth):
        w_perm = _permute_w_in(w_in[l])
        bmat, cmat, lre, lim = _ssm_params(lambda_re[l], lambda_im[l], log_dt[l], b_re[l], b_im[l],
                                           c_re[l], c_im[l])
        o1, o2, o3, pa = _in_projection(x2d.reshape(bsz, seq, d_model), pre_norm_g[l][None, :], w_perm)
        uz = pa[:, 2 * d_model:2 * d_model + 2 * SSM_WIDTH]
        uz_tm = uz.reshape(bsz, seq, 2 * SSM_WIDTH).transpose(1, 0, 2).reshape(seq * bsz, 2 * SSM_WIDTH)
        ys_tm = _s5_branch(uz_tm, bmat, cmat, lre, lim, d_skip[l][None, :], w_glu[l].astype(BF16),
                           b_glu[l][None, :], bsz=bsz)
        ys = ys_tm.reshape(seq, bsz, SSM_WIDTH).transpose(1, 0, 2).reshape(bsz * seq, SSM_WIDTH)
        ya = _attention(o1, o2, o3).reshape(bsz * seq, ATTN_WIDTH)
        x2d = _merge(x2d, ys, ya, pa, w_branch_s[l].astype(BF16), w_branch_a[l].astype(BF16),
                     w_out[l].astype(BF16), post_norm_g[l][None, :])
    return x2d.reshape(bsz, seq, d_model)
```

```python
import functools

import jax
import jax.numpy as jnp
from jax import lax
from jax.experimental import pallas as pl
from jax.experimental.pallas import tpu as pltpu

F32 = jnp.float32
BF16 = jnp.bfloat16

RMS_EPS = 1e-6
SSM_WIDTH = 512
SSM_GROUP = 16
SSM_GROUPS = 32
SSM_STATE = 64
N_STATE = SSM_GROUPS * SSM_STATE
HEAD_DIM = 128
HEADS_PER_GROUP = 4
SPAN = 128
ATTN_WIDTH = HEADS_PER_GROUP * HEAD_DIM
LANES = 128
VMEM_LIMIT = 56 * 1024 * 1024

PROJ_TN = 512
N_QKV_TILES = 9
N_PROJ_TILES = 16
ATTN_BATCH = 8


def _rmsnorm_f32(x, g):
    inv = lax.rsqrt(jnp.mean(x * x, axis=-1, keepdims=True) + RMS_EPS)
    return x * inv * g


def _proj_kernel(x_ref, g_ref, w_ref, o1_ref, o2_ref, o3_ref, pa_ref,
                 hf_ref, hn_ref, h4_ref, h16_ref, *, tm):
    d_model = x_ref.shape[-1]
    n_slab = d_model // LANES
    hf = _rmsnorm_f32(x_ref[0], g_ref[...])
    hn_ref[...] = hf.astype(BF16)
    for k in range(n_slab):
        hf_ref[k] = hf[:, k * LANES:(k + 1) * LANES]
    for r, dst in ((4, h4_ref), (16, h16_ref)):
        rows = tm // r
        for c in range(r):
            for k in range(n_slab):
                dst[c * rows:(c + 1) * rows, k * LANES:(k + 1) * LANES] = (
                    hf_ref[k, pl.ds(c, rows, stride=r), :].astype(BF16))

    def tile(h_ref, j):
        return jnp.dot(h_ref[...], w_ref[:, j * PROJ_TN:(j + 1) * PROJ_TN], preferred_element_type=F32)

    for j in range(3):
        cols = slice(j * PROJ_TN, (j + 1) * PROJ_TN)
        o1_ref[0, 0, :, cols] = tile(hn_ref, j).astype(BF16)
        o2_ref[0, :, :, cols] = tile(h4_ref, 3 + j).astype(BF16).reshape(4, tm // 4, PROJ_TN)
        o3_ref[0, :, :, cols] = tile(h16_ref, 6 + j).astype(BF16).reshape(16, tm // 16, PROJ_TN)
    for j in range(N_QKV_TILES, N_PROJ_TILES):
        jj = j - N_QKV_TILES
        pa_ref[:, jj * PROJ_TN:(jj + 1) * PROJ_TN] = tile(hn_ref, j)


def _in_projection(x, g, w_perm, *, tm=512):
    bsz, seq, d_model = x.shape
    n_i = seq // tm
    kern = functools.partial(_proj_kernel, tm=tm)
    n_pa = N_PROJ_TILES - N_QKV_TILES
    qkv_w = 3 * ATTN_WIDTH
    return pl.pallas_call(
        kern,
        grid=(bsz, n_i),
        in_specs=[
            pl.BlockSpec((1, tm, d_model), lambda b, i: (b, i, 0)),
            pl.BlockSpec((1, d_model), lambda b, i: (0, 0)),
            pl.BlockSpec(w_perm.shape, lambda b, i: (0, 0), pipeline_mode=pl.Buffered(1)),
        ],
        out_specs=[
            pl.BlockSpec((1, 1, tm, qkv_w), lambda b, i: (b, 0, i, 0)),
            pl.BlockSpec((1, 4, tm // 4, qkv_w), lambda b, i: (b, 0, i, 0)),
            pl.BlockSpec((1, 16, tm // 16, qkv_w), lambda b, i: (b, 0, i, 0)),
            pl.BlockSpec((tm, n_pa * PROJ_TN), lambda b, i: (b * n_i + i, 0)),
        ],
        out_shape=[
            jax.ShapeDtypeStruct((bsz, 1, seq, qkv_w), BF16),
            jax.ShapeDtypeStruct((bsz, 4, seq // 4, qkv_w), BF16),
            jax.ShapeDtypeStruct((bsz, 16, seq // 16, qkv_w), BF16),
            jax.ShapeDtypeStruct((bsz * seq, n_pa * PROJ_TN), F32),
        ],
        scratch_shapes=[
            pltpu.VMEM((d_model // LANES, tm, LANES), F32),
            pltpu.VMEM((tm, d_model), BF16),
            pltpu.VMEM((tm, d_model), BF16),
            pltpu.VMEM((tm, d_model), BF16),
        ],
        compiler_params=pltpu.CompilerParams(
            dimension_semantics=("arbitrary", "arbitrary"),
            vmem_limit_bytes=VMEM_LIMIT),
        name="in_projection",
    )(x, g, w_perm)


def _s5_kernel(uz_ref, bmat_ref, cmat_ref, lre_ref, lim_ref, dskip_ref, wglu_ref, bglu_ref,
               ys_ref, drive_ref, state_ref, *, tt, bsz, lane_chunk):
    @pl.when(pl.program_id(0) == 0)
    def _():
        state_ref[...] = jnp.zeros_like(state_ref)

    u = uz_ref[:, :SSM_WIDTH]
    z = uz_ref[:, SSM_WIDTH:]
    drive_ref[...] = jnp.dot(u.astype(BF16), bmat_ref[...], preferred_element_type=F32)

    for lo in range(0, N_STATE, lane_chunk):
        re_sl = slice(lo, lo + lane_chunk)
        im_sl = slice(N_STATE + lo, N_STATE + lo + lane_chunk)
        lre = jnp.broadcast_to(lre_ref[:, re_sl], (bsz, lane_chunk))
        lim = jnp.broadcast_to(lim_ref[:, re_sl], (bsz, lane_chunk))

        def step(t, carry, re_sl=re_sl, im_sl=im_sl, lre=lre, lim=lim):
            hre, him = carry
            rows = pl.ds(pl.multiple_of(t * bsz, bsz), bsz)
            nre = lre * hre - lim * him + drive_ref[rows, re_sl]
            nim = lre * him + lim * hre + drive_ref[rows, im_sl]
            drive_ref[rows, re_sl] = nre
            drive_ref[rows, im_sl] = nim
            return nre, nim

        hre, him = lax.fori_loop(0, tt, step, (state_ref[:, re_sl], state_ref[:, im_sl]))
        state_ref[:, re_sl] = hre
        state_ref[:, im_sl] = him

    y = jnp.dot(drive_ref[...].astype(BF16), cmat_ref[...], preferred_element_type=F32)
    y = y + dskip_ref[...] * u
    y = jax.nn.gelu(y)
    gate = jnp.dot(y.astype(BF16), wglu_ref[...], preferred_element_type=F32) + bglu_ref[...]
    y = y * jax.nn.sigmoid(gate)
    ys_ref[...] = y * (z * jax.nn.sigmoid(z))


def _s5_branch(uz_tm, bmat, cmat, lre, lim, d_skip, w_glu, b_glu, *, bsz, tt=64, lane_chunk=512):
    rows_total = uz_tm.shape[0]
    rows = tt * bsz
    kern = functools.partial(_s5_kernel, tt=tt, bsz=bsz, lane_chunk=lane_chunk)
    full = lambda shape: pl.BlockSpec(shape, lambda i: (0,) * len(shape))
    return pl.pallas_call(
        kern,
        grid=(rows_total // rows,),
        in_specs=[
            pl.BlockSpec((rows, 2 * SSM_WIDTH), lambda i: (i, 0)),
            full(bmat.shape), full(cmat.shape), full(lre.shape), full(lim.shape),
            full(d_skip.shape), full(w_glu.shape), full(b_glu.shape),
        ],
        out_specs=pl.BlockSpec((rows, SSM_WIDTH), lambda i: (i, 0)),
        out_shape=jax.ShapeDtypeStruct((rows_total, SSM_WIDTH), F32),
        scratch_shapes=[
            pltpu.VMEM((rows, 2 * N_STATE), F32),
            pltpu.VMEM((bsz, 2 * N_STATE), F32),
        ],
        compiler_params=pltpu.CompilerParams(
            dimension_semantics=("arbitrary",), vmem_limit_bytes=VMEM_LIMIT),
        name="s5_branch",
    )(uz_tm, bmat, cmat, lre, lim, d_skip, w_glu, b_glu)


def _attn_blocks(q, k, v, first):
    s = jnp.einsum('gqd,gkd->gqk', q, k, preferred_element_type=F32) * (HEAD_DIM ** -0.5)
    qi = lax.broadcasted_iota(jnp.int32, s.shape[1:], 0)
    ki = lax.broadcasted_iota(jnp.int32, s.shape[1:], 1)
    valid = (ki <= qi) if first else ((ki >= qi) & (ki <= qi + SPAN))
    s = jnp.where(valid[None], s, -jnp.inf)
    m = jnp.max(s, axis=-1, keepdims=True)
    p = jnp.exp(s - m)
    l = jnp.sum(p, axis=-1, keepdims=True)
    acc = jnp.einsum('gqk,gkd->gqd', p.astype(BF16), v, preferred_element_type=F32)
    return acc / l, m + jnp.log(l)


def _attn_kernel(q1_ref, k1_ref, v1_ref, q2_ref, k2_ref, v2_ref, q3_ref, k3_ref, v3_ref,
                 y_ref, out_ref, lse_ref):
    groups = ((q1_ref, k1_ref, v1_ref), (q2_ref, k2_ref, v2_ref), (q3_ref, k3_ref, v3_ref))
    for gi, (q_ref, k_ref, v_ref) in enumerate(groups):
        r = q_ref.shape[1]
        n_blk = q_ref.shape[2] // SPAN

        def put(blocks, o, lse, gi=gi, r=r):
            for g, (c, n) in enumerate(blocks):
                if r == 1:
                    rows = pl.ds(n * SPAN, SPAN)
                else:
                    rows = pl.ds(c + r * SPAN * n, SPAN, stride=r)
                out_ref[gi, rows, :] = o[g]
                lse_ref[gi, rows, :] = jnp.broadcast_to(lse[g], o[g].shape)

        first = [(c, 0) for c in range(r)]
        rest = [(c, n) for c in range(r) for n in range(1, n_blk)]
        for blocks, is_first in ((first, True), (rest, False)):
            for lo in range(0, len(blocks), ATTN_BATCH):
                chunk = blocks[lo:lo + ATTN_BATCH]
                q = jnp.stack([q_ref[0, c, n * SPAN:(n + 1) * SPAN, :] for c, n in chunk])
                k0 = 0 if is_first else 1
                k = jnp.stack([k_ref[0, c, (n - k0) * SPAN:(n + 1) * SPAN, :] for c, n in chunk])
                v = jnp.stack([v_ref[0, c, (n - k0) * SPAN:(n + 1) * SPAN, :] for c, n in chunk])
                o, lse = _attn_blocks(q, k, v, is_first)
                put(chunk, o, lse)

    lse = lse_ref[...]
    mx = jnp.max(lse, axis=0)
    w = jnp.exp(lse - mx[None])
    y_ref[0] = jnp.sum(w * out_ref[...], axis=0) / jnp.sum(w, axis=0)


def _attention(o1, o2, o3):
    bsz, _, seq, _ = o1.shape

    def specs(arr):
        _, r, n, _ = arr.shape
        return [pl.BlockSpec((1, r, n, HEAD_DIM), lambda b, h, part=part: (b, 0, 0, part * HEADS_PER_GROUP + h))
                for part in range(3)]

    return pl.pallas_call(
        _attn_kernel,
        grid=(bsz, HEADS_PER_GROUP),
        in_specs=specs(o1) + specs(o2) + specs(o3),
        out_specs=pl.BlockSpec((1, seq, HEAD_DIM), lambda b, h: (b, 0, h)),
        out_shape=jax.ShapeDtypeStruct((bsz, seq, ATTN_WIDTH), F32),
        scratch_shapes=[
            pltpu.VMEM((3, seq, HEAD_DIM), F32),
            pltpu.VMEM((3, seq, HEAD_DIM), F32),
        ],
        compiler_params=pltpu.CompilerParams(
            dimension_semantics=("arbitrary", "arbitrary"), vmem_limit_bytes=VMEM_LIMIT),
        name="dilated_attention",
    )(o1, o1, o1, o2, o2, o2, o3, o3, o3)


def _merge_kernel(x_ref, ys_ref, ya_ref, za_ref, gs_ref, ga_ref, wbs_ref, wba_ref, wout_ref, pg_ref,
                  o_ref):
    za = za_ref[...]
    ya = ya_ref[...] * (za * jax.nn.sigmoid(za))
    ms = jnp.dot(ys_ref[...].astype(BF16), wbs_ref[...], preferred_element_type=F32)
    ma = jnp.dot(ya.astype(BF16), wba_ref[...], preferred_element_type=F32)
    merged = jax.nn.sigmoid(gs_ref[...]) * ms + jax.nn.sigmoid(ga_ref[...]) * ma
    out = jnp.dot(merged.astype(BF16), wout_ref[...], preferred_element_type=F32)
    o_ref[...] = x_ref[...] + _rmsnorm_f32(out, pg_ref[...])


def _merge(x2d, ys, ya, pa, wbs, wba, wout, pg, *, tm=512):
    rows, d_model = x2d.shape
    full = lambda shape: pl.BlockSpec(shape, lambda i: (0,) * len(shape))
    return pl.pallas_call(
        _merge_kernel,
        grid=(rows // tm,),
        in_specs=[
            pl.BlockSpec((tm, d_model), lambda i: (i, 0)),
            pl.BlockSpec((tm, SSM_WIDTH), lambda i: (i, 0)),
            pl.BlockSpec((tm, ATTN_WIDTH), lambda i: (i, 0)),
            pl.BlockSpec((tm, ATTN_WIDTH), lambda i: (i, 6)),
            pl.BlockSpec((tm, d_model), lambda i: (i, 0)),
            pl.BlockSpec((tm, d_model), lambda i: (i, 1)),
            full(wbs.shape), full(wba.shape), full(wout.shape), full(pg.shape),
        ],
        out_specs=pl.BlockSpec((tm, d_model), lambda i: (i, 0)),
        out_shape=jax.ShapeDtypeStruct((rows, d_model), F32),
        compiler_params=pltpu.CompilerParams(
            dimension_semantics=("arbitrary",), vmem_limit_bytes=VMEM_LIMIT),
        name="merge_out",
    )(x2d, ys, ya, pa, pa, pa, wbs, wba, wout, pg)


def _permute_w_in(w):
    s, a = SSM_WIDTH, ATTN_WIDTH
    q0, k0, v0 = 2 * s, 2 * s + 3 * a, 2 * s + 6 * a
    za0 = 2 * s + 9 * a
    gs0 = za0 + a
    d = w.shape[0]
    ga0 = gs0 + d
    cols = []
    for gi in range(3):
        for base in (q0, k0, v0):
            cols.append(w[:, base + gi * a: base + (gi + 1) * a])
    cols += [w[:, gs0:gs0 + d], w[:, ga0:ga0 + d], w[:, 0:s], w[:, s:2 * s], w[:, za0:za0 + a]]
    return jnp.concatenate(cols, axis=1).astype(BF16)


def _ssm_params(lam_re, lam_im, log_dt, b_re, b_im, c_re, c_im):
    a = jnp.minimum(lam_re, -1e-4)
    b = lam_im
    dt = jnp.exp(log_dt)[:, None]
    mag = jnp.exp(a * dt)
    lbr = mag * jnp.cos(b * dt)
    lbi = mag * jnp.sin(b * dt)
    nr, ni = lbr - 1.0, lbi
    den = a * a + b * b
    kr = (nr * a + ni * b) / den
    ki = (ni * a - nr * b) / den
    bbr = kr[..., None] * b_re - ki[..., None] * b_im
    bbi = kr[..., None] * b_im + ki[..., None] * b_re
    eye = jnp.eye(SSM_GROUPS, dtype=F32)

    def blk_in(m):
        return jnp.einsum('gpc,gh->gchp', m, eye).reshape(SSM_WIDTH, N_STATE)

    def blk_out(m):
        return jnp.einsum('gcp,gh->gphc', m, eye).reshape(N_STATE, SSM_WIDTH)

    bmat = jnp.concatenate([blk_in(bbr), blk_in(bbi)], axis=1).astype(BF16)
    cmat = jnp.concatenate([blk_out(c_re), -blk_out(c_im)], axis=0).astype(BF16)
    return bmat, cmat, lbr.reshape(1, N_STATE), lbi.reshape(1, N_STATE)


def kernel(x, pre_norm_g, w_in, lambda_re, lambda_im, log_dt, b_re, b_im, c_re, c_im, d_skip, w_glu, b_glu, w_branch_s, w_branch_a, w_out, post_norm_g):
    bsz, seq, d_model = x.shape
    depth = w_in.shape[0]
    x2d = x.reshape(bsz * seq, d_model)
    for l in range(depth):
        w_perm = _permute_w_in(w_in[l])
        bmat, cmat, lre, lim = _ssm_params(lambda_re[l], lambda_im[l], log_dt[l], b_re[l], b_im[l],
                                           c_re[l], c_im[l])
        o1, o2, o3, pa = _in_projection(x2d.reshape(bsz, seq, d_model), pre_norm_g[l][None, :], w_perm)
        uz = pa[:, 2 * d_model:2 * d_model + 2 * SSM_WIDTH]
        uz_tm = uz.reshape(bsz, seq, 2 * SSM_WIDTH).transpose(1, 0, 2).reshape(seq * bsz, 2 * SSM_WIDTH)
        ys_tm = _s5_branch(uz_tm, bmat, cmat, lre, lim, d_skip[l][None, :], w_glu[l].astype(BF16),
                           b_glu[l][None, :], bsz=bsz)
        ys = ys_tm.reshape(seq, bsz, SSM_WIDTH).transpose(1, 0, 2).reshape(bsz * seq, SSM_WIDTH)
        ya = _attention(o1, o2, o3).reshape(bsz * seq, ATTN_WIDTH)
        x2d = _merge(x2d, ys, ya, pa, w_branch_s[l].astype(BF16), w_branch_a[l].astype(BF16),
                     w_out[l].astype(BF16), post_norm_g[l][None, :])
    return x2d.reshape(bsz, seq, d_model)
```

```python
import functools

import jax
import jax.numpy as jnp
from jax import lax
from jax.experimental import pallas as pl
from jax.experimental.pallas import tpu as pltpu

F32 = jnp.float32
BF16 = jnp.bfloat16

RMS_EPS = 1e-6
SSM_WIDTH = 512
SSM_GROUP = 16
SSM_GROUPS = 32
SSM_STATE = 64
N_STATE = SSM_GROUPS * SSM_STATE
HEAD_DIM = 128
HEADS_PER_GROUP = 4
SPAN = 128
ATTN_WIDTH = HEADS_PER_GROUP * HEAD_DIM
LANES = 128
VMEM_LIMIT = 56 * 1024 * 1024

PROJ_TN = 512
D_MODEL = 1024
COL_U, COL_ZS = 0, SSM_WIDTH
COL_Q = 2 * SSM_WIDTH
COL_K = COL_Q + 3 * ATTN_WIDTH
COL_V = COL_K + 3 * ATTN_WIDTH
COL_ZA = COL_V + 3 * ATTN_WIDTH
COL_GS = COL_ZA + ATTN_WIDTH
COL_GA = COL_GS + D_MODEL
PA_SRC = (COL_GS, COL_GS + PROJ_TN, COL_GA, COL_GA + PROJ_TN, COL_U, COL_ZS, COL_ZA)
PA_WIDTH = len(PA_SRC) * PROJ_TN
PA_UZ_BLOCK = 2
PA_ZA_BLOCK = 6
ATTN_BATCH = 8
STATE_TILE = 2 * LANES
N_STATE_TILES = 2 * N_STATE // STATE_TILE


def _rmsnorm_f32(x, g):
    inv = lax.rsqrt(jnp.mean(x * x, axis=-1, keepdims=True) + RMS_EPS)
    return x * inv * g


def _proj_kernel(x_ref, g_ref, w_ref, o1_ref, o2_ref, o3_ref, pa_ref,
                 hf_ref, hn_ref, h4_ref, h16_ref, *, tm):
    d_model = x_ref.shape[-1]
    n_slab = d_model // LANES
    hf = _rmsnorm_f32(x_ref[0], g_ref[0])
    hn_ref[...] = hf.astype(BF16)
    for k in range(n_slab):
        hf_ref[k] = hf[:, k * LANES:(k + 1) * LANES]
    for r, dst in ((4, h4_ref), (16, h16_ref)):
        rows = tm // r
        for c in range(r):
            for k in range(n_slab):
                dst[c * rows:(c + 1) * rows, k * LANES:(k + 1) * LANES] = (
                    hf_ref[k, pl.ds(c, rows, stride=r), :].astype(BF16))

    def tile(h_ref, col):
        return jnp.dot(h_ref[...], w_ref[0, :, col:col + PROJ_TN], preferred_element_type=F32)

    for j, base in enumerate((COL_Q, COL_K, COL_V)):
        cols = slice(j * PROJ_TN, (j + 1) * PROJ_TN)
        o1_ref[0, 0, :, cols] = tile(hn_ref, base).astype(BF16)
        o2_ref[0, :, :, cols] = tile(h4_ref, base + ATTN_WIDTH).astype(BF16).reshape(
            4, tm // 4, PROJ_TN)
        o3_ref[0, :, :, cols] = tile(h16_ref, base + 2 * ATTN_WIDTH).astype(BF16).reshape(
            16, tm // 16, PROJ_TN)
    for jj, col in enumerate(PA_SRC):
        pa_ref[0, :, jj * PROJ_TN:(jj + 1) * PROJ_TN] = tile(hn_ref, col)


def _in_projection(x, g_all, w_all, layer, *, tm=512):
    bsz, seq, d_model = x.shape
    n_i = seq // tm
    kern = functools.partial(_proj_kernel, tm=tm)
    qkv_w = 3 * ATTN_WIDTH
    return pl.pallas_call(
        kern,
        grid=(bsz, n_i),
        in_specs=[
            pl.BlockSpec((1, tm, d_model), lambda b, i: (b, i, 0)),
            pl.BlockSpec((1, 1, d_model), lambda b, i: (layer, 0, 0)),
            pl.BlockSpec((1,) + w_all.shape[1:], lambda b, i: (layer, 0, 0),
                         pipeline_mode=pl.Buffered(1)),
        ],
        out_specs=[
            pl.BlockSpec((1, 1, tm, qkv_w), lambda b, i: (b, 0, i, 0)),
            pl.BlockSpec((1, 4, tm // 4, qkv_w), lambda b, i: (b, 0, i, 0)),
            pl.BlockSpec((1, 16, tm // 16, qkv_w), lambda b, i: (b, 0, i, 0)),
            pl.BlockSpec((1, tm, PA_WIDTH), lambda b, i: (b, i, 0)),
        ],
        out_shape=[
            jax.ShapeDtypeStruct((bsz, 1, seq, qkv_w), BF16),
            jax.ShapeDtypeStruct((bsz, 4, seq // 4, qkv_w), BF16),
            jax.ShapeDtypeStruct((bsz, 16, seq // 16, qkv_w), BF16),
            jax.ShapeDtypeStruct((bsz, seq, PA_WIDTH), F32),
        ],
        scratch_shapes=[
            pltpu.VMEM((d_model // LANES, tm, LANES), F32),
            pltpu.VMEM((tm, d_model), BF16),
            pltpu.VMEM((tm, d_model), BF16),
            pltpu.VMEM((tm, d_model), BF16),
        ],
        compiler_params=pltpu.CompilerParams(
            dimension_semantics=("arbitrary", "arbitrary"),
            vmem_limit_bytes=VMEM_LIMIT),
        name="in_projection",
    )(x, g_all, w_all)


def _s5_kernel(uz_ref, bmat_ref, cmat_ref, lre_ref, lim_ref, dskip_ref, wglu_ref, bglu_ref,
               ys_ref, drive_ref, h_ref, state_ref, *, tt, bsz, tiles_per_chunk, scan_unroll):
    @pl.when(pl.program_id(0) == 0)
    def _():
        state_ref[...] = jnp.zeros_like(state_ref)

    rows = tt * bsz
    u = jnp.swapaxes(uz_ref[:, :, :SSM_WIDTH], 0, 1).reshape(rows, SSM_WIDTH)
    u_bf = u.astype(BF16)
    for n in range(N_STATE_TILES):
        blk = n // tiles_per_chunk
        drive_ref[:, n * STATE_TILE:(n + 1) * STATE_TILE] = jnp.dot(
            u_bf[:, blk * LANES:(blk + 1) * LANES], bmat_ref[n], preferred_element_type=F32)

    for n0 in range(0, N_STATE_TILES, tiles_per_chunk):
        tiles = range(n0, n0 + tiles_per_chunk)
        lam = [(jnp.broadcast_to(lre_ref[:, n * LANES:(n + 1) * LANES], (bsz, LANES)),
                jnp.broadcast_to(lim_ref[:, n * LANES:(n + 1) * LANES], (bsz, LANES))) for n in tiles]

        def step(t, carry, tiles=tiles, lam=lam):
            rows_t = pl.ds(pl.multiple_of(t * bsz, bsz), bsz)
            out = []
            for (hre, him), (lre, lim), n in zip(carry, lam, tiles):
                re_sl = slice(n * STATE_TILE, n * STATE_TILE + LANES)
                im_sl = slice(n * STATE_TILE + LANES, (n + 1) * STATE_TILE)
                nre = lre * hre - lim * him + drive_ref[rows_t, re_sl]
                nim = lre * him + lim * hre + drive_ref[rows_t, im_sl]
                h_ref[rows_t, re_sl] = nre
                h_ref[rows_t, im_sl] = nim
                out.append((nre, nim))
            return tuple(out)

        init = tuple((state_ref[:, n * STATE_TILE:n * STATE_TILE + LANES],
                      state_ref[:, n * STATE_TILE + LANES:(n + 1) * STATE_TILE]) for n in tiles)
        final = lax.fori_loop(0, tt, step, init, unroll=scan_unroll)
        for (hre, him), n in zip(final, tiles):
            state_ref[:, n * STATE_TILE:n * STATE_TILE + LANES] = hre
            state_ref[:, n * STATE_TILE + LANES:(n + 1) * STATE_TILE] = him

    chunk = tiles_per_chunk * STATE_TILE
    y = jnp.concatenate(
        [jnp.dot(h_ref[:, m * chunk:(m + 1) * chunk].astype(BF16), cmat_ref[m],
                 preferred_element_type=F32) for m in range(SSM_WIDTH // LANES)], axis=1)
    y = y + dskip_ref[...] * u
    y = jax.nn.gelu(y)
    gate = jnp.dot(y.astype(BF16), wglu_ref[0], preferred_element_type=F32) + bglu_ref[...]
    y = y * jax.nn.sigmoid(gate)
    z = uz_ref[:, :, SSM_WIDTH:]
    ys_ref[...] = jnp.swapaxes(y.reshape(tt, bsz, SSM_WIDTH), 0, 1) * (z * jax.nn.sigmoid(z))


def _s5_branch(pa, bmat, cmat, lre, lim, d_skip, w_glu_all, b_glu, layer, *, tt=64, scan_unroll=8):
    bsz, seq, _ = pa.shape
    tiles_per_chunk = LANES // (2 * SSM_GROUP)
    kern = functools.partial(_s5_kernel, tt=tt, bsz=bsz, tiles_per_chunk=tiles_per_chunk,
                             scan_unroll=scan_unroll)
    full = lambda shape: pl.BlockSpec(shape, lambda i: (0,) * len(shape))
    return pl.pallas_call(
        kern,
        grid=(seq // tt,),
        in_specs=[
            pl.BlockSpec((bsz, tt, 2 * SSM_WIDTH), lambda i: (0, i, PA_UZ_BLOCK)),
            full(bmat.shape), full(cmat.shape), full(lre.shape), full(lim.shape),
            full(d_skip.shape),
            pl.BlockSpec((1,) + w_glu_all.shape[1:], lambda i: (layer, 0, 0)),
            full(b_glu.shape),
        ],
        out_specs=pl.BlockSpec((bsz, tt, SSM_WIDTH), lambda i: (0, i, 0)),
        out_shape=jax.ShapeDtypeStruct((bsz, seq, SSM_WIDTH), F32),
        scratch_shapes=[
            pltpu.VMEM((tt * bsz, 2 * N_STATE), F32),
            pltpu.VMEM((tt * bsz, 2 * N_STATE), F32),
            pltpu.VMEM((bsz, 2 * N_STATE), F32),
        ],
        compiler_params=pltpu.CompilerParams(
            dimension_semantics=("arbitrary",), vmem_limit_bytes=VMEM_LIMIT),
        name="s5_branch",
    )(pa, bmat, cmat, lre, lim, d_skip, w_glu_all, b_glu)


def _attn_blocks(q, k, v, first):
    s = jnp.einsum('gqd,gkd->gqk', q, k, preferred_element_type=F32) * (HEAD_DIM ** -0.5)
    qi = lax.broadcasted_iota(jnp.int32, s.shape[1:], 0)
    ki = lax.broadcasted_iota(jnp.int32, s.shape[1:], 1)
    valid = (ki <= qi) if first else ((ki >= qi) & (ki <= qi + SPAN))
    s = jnp.where(valid[None], s, -jnp.inf)
    m = jnp.max(s, axis=-1, keepdims=True)
    p = jnp.exp(s - m)
    l = jnp.sum(p, axis=-1, keepdims=True)
    acc = jnp.einsum('gqk,gkd->gqd', p.astype(BF16), v, preferred_element_type=F32)
    return acc / l, m + jnp.log(l)


def _attn_kernel(q1_ref, k1_ref, v1_ref, q2_ref, k2_ref, v2_ref, q3_ref, k3_ref, v3_ref,
                 y_ref, out_ref, lse_ref):
    groups = ((q1_ref, k1_ref, v1_ref), (q2_ref, k2_ref, v2_ref), (q3_ref, k3_ref, v3_ref))
    for gi, (q_ref, k_ref, v_ref) in enumerate(groups):
        r = q_ref.shape[1]
        n_blk = q_ref.shape[2] // SPAN

        def put(blocks, o, lse, gi=gi, r=r):
            for g, (c, n) in enumerate(blocks):
                if r == 1:
                    rows = pl.ds(n * SPAN, SPAN)
                else:
                    rows = pl.ds(c + r * SPAN * n, SPAN, stride=r)
                out_ref[gi, rows, :] = o[g]
                lse_ref[gi, rows, :] = jnp.broadcast_to(lse[g], o[g].shape)

        first = [(c, 0) for c in range(r)]
        rest = [(c, n) for c in range(r) for n in range(1, n_blk)]
        for blocks, is_first in ((first, True), (rest, False)):
            for lo in range(0, len(blocks), ATTN_BATCH):
                chunk = blocks[lo:lo + ATTN_BATCH]
                q = jnp.stack([q_ref[0, c, n * SPAN:(n + 1) * SPAN, :] for c, n in chunk])
                k0 = 0 if is_first else 1
                k = jnp.stack([k_ref[0, c, (n - k0) * SPAN:(n + 1) * SPAN, :] for c, n in chunk])
                v = jnp.stack([v_ref[0, c, (n - k0) * SPAN:(n + 1) * SPAN, :] for c, n in chunk])
                o, lse = _attn_blocks(q, k, v, is_first)
                put(chunk, o, lse)

    lse = lse_ref[...]
    mx = jnp.max(lse, axis=0)
    w = jnp.exp(lse - mx[None])
    y_ref[0] = jnp.sum(w * out_ref[...], axis=0) / jnp.sum(w, axis=0)


def _attention(o1, o2, o3):
    bsz, _, seq, _ = o1.shape

    def specs(arr):
        _, r, n, _ = arr.shape
        return [pl.BlockSpec((1, r, n, HEAD_DIM), lambda b, h, part=part: (b, 0, 0, part * HEADS_PER_GROUP + h))
                for part in range(3)]

    return pl.pallas_call(
        _attn_kernel,
        grid=(bsz, HEADS_PER_GROUP),
        in_specs=specs(o1) + specs(o2) + specs(o3),
        out_specs=pl.BlockSpec((1, seq, HEAD_DIM), lambda b, h: (b, 0, h)),
        out_shape=jax.ShapeDtypeStruct((bsz, seq, ATTN_WIDTH), F32),
        scratch_shapes=[
            pltpu.VMEM((3, seq, HEAD_DIM), F32),
            pltpu.VMEM((3, seq, HEAD_DIM), F32),
        ],
        compiler_params=pltpu.CompilerParams(
            dimension_semantics=("arbitrary", "arbitrary"), vmem_limit_bytes=VMEM_LIMIT),
        name="dilated_attention",
    )(o1, o1, o1, o2, o2, o2, o3, o3, o3)


def _merge_kernel(x_ref, ys_ref, ya_ref, za_ref, gs_ref, ga_ref, wbs_ref, wba_ref, wout_ref, pg_ref,
                  o_ref):
    za = za_ref[0]
    ya = ya_ref[0] * (za * jax.nn.sigmoid(za))
    ms = jnp.dot(ys_ref[0].astype(BF16), wbs_ref[0], preferred_element_type=F32)
    ma = jnp.dot(ya.astype(BF16), wba_ref[0], preferred_element_type=F32)
    merged = jax.nn.sigmoid(gs_ref[0]) * ms + jax.nn.sigmoid(ga_ref[0]) * ma
    out = jnp.dot(merged.astype(BF16), wout_ref[0], preferred_element_type=F32)
    o_ref[0] = x_ref[0] + _rmsnorm_f32(out, pg_ref[0])


def _merge(x, ys, ya, pa, wbs_all, wba_all, wout_all, pg_all, layer, *, tm=512):
    bsz, seq, d_model = x.shape
    row = lambda width, blk: pl.BlockSpec((1, tm, width), lambda b, i: (b, i, blk))
    per_layer = lambda arr: pl.BlockSpec((1,) + arr.shape[1:], lambda b, i: (layer, 0, 0))
    return pl.pallas_call(
        _merge_kernel,
        grid=(bsz, seq // tm),
        in_specs=[
            row(d_model, 0), row(SSM_WIDTH, 0), row(ATTN_WIDTH, 0),
            row(ATTN_WIDTH, PA_ZA_BLOCK),
            row(d_model, 0),
            row(d_model, 1),
            per_layer(wbs_all), per_layer(wba_all), per_layer(wout_all), per_layer(pg_all),
        ],
        out_specs=row(d_model, 0),
        out_shape=jax.ShapeDtypeStruct((bsz, seq, d_model), F32),
        compiler_params=pltpu.CompilerParams(
            dimension_semantics=("arbitrary", "arbitrary"), vmem_limit_bytes=VMEM_LIMIT),
        name="merge_out",
    )(x, ys, ya, pa, pa, pa, wbs_all, wba_all, wout_all, pg_all)


def _ssm_params(lam_re, lam_im, log_dt, b_re, b_im, c_re, c_im):
    depth = lam_re.shape[0]
    a = jnp.minimum(lam_re, -1e-4)
    b = lam_im
    dt = jnp.exp(log_dt)[..., None]
    mag = jnp.exp(a * dt)
    lbr = mag * jnp.cos(b * dt)
    lbi = mag * jnp.sin(b * dt)
    nr, ni = lbr - 1.0, lbi
    den = a * a + b * b
    kr = (nr * a + ni * b) / den
    ki = (ni * a - nr * b) / den
    bbr = kr[..., None] * b_re - ki[..., None] * b_im
    bbi = kr[..., None] * b_im + ki[..., None] * b_re
    per_block = LANES // (2 * SSM_GROUP)
    eye2 = jnp.eye(2, dtype=F32)
    slot = jax.nn.one_hot(jnp.arange(N_STATE_TILES) % per_block, per_block, dtype=F32)
    b5 = jnp.stack([bbr, bbi], axis=1).reshape(depth, 2, N_STATE_TILES, 2, SSM_STATE, SSM_GROUP)
    bblk = jnp.einsum('langpc,gh->lngcahp', b5, eye2).reshape(
        depth, N_STATE_TILES, 2 * SSM_GROUP, STATE_TILE)
    bmat = (bblk[:, :, None] * slot[None, :, :, None, None]).reshape(
        depth, N_STATE_TILES, LANES, STATE_TILE).astype(BF16)
    c5 = jnp.stack([c_re, -c_im], axis=1).reshape(depth, 2, N_STATE_TILES, 2, SSM_GROUP, SSM_STATE)
    cblk = jnp.einsum('langcp,gh->lnagphc', c5, eye2).reshape(
        depth, N_STATE_TILES, STATE_TILE, 2 * SSM_GROUP)
    cmat = (cblk[:, :, :, None, :] * slot[None, :, None, :, None]).reshape(
        depth, N_STATE_TILES // per_block, per_block * STATE_TILE, LANES).astype(BF16)
    return bmat, cmat, lbr.reshape(depth, 1, N_STATE), lbi.reshape(depth, 1, N_STATE)


def kernel(x, pre_norm_g, w_in, lambda_re, lambda_im, log_dt, b_re, b_im, c_re, c_im, d_skip, w_glu, b_glu, w_branch_s, w_branch_a, w_out, post_norm_g):
    depth = w_in.shape[0]
    assert x.shape[-1] == D_MODEL and w_in.shape[-1] == COL_GA + D_MODEL
    w_in_bf = w_in.astype(BF16)
    w_glu_bf = w_glu.astype(BF16)
    wbs_bf, wba_bf, wout_bf = (w.astype(BF16) for w in (w_branch_s, w_branch_a, w_out))
    pre_g = pre_norm_g[:, None, :]
    post_g = post_norm_g[:, None, :]
    bmat, cmat, lre, lim = _ssm_params(lambda_re, lambda_im, log_dt, b_re, b_im, c_re, c_im)
    for l in range(depth):
        o1, o2, o3, pa = _in_projection(x, pre_g, w_in_bf, l)
        ys = _s5_branch(pa, bmat[l], cmat[l], lre[l], lim[l], d_skip[l][None, :], w_glu_bf,
                        b_glu[l][None, :], l)
        ya = _attention(o1, o2, o3)
        x = _merge(x, ys, ya, pa, wbs_bf, wba_bf, wout_bf, post_g, l)
    return x
```

```python
import functools

import jax
import jax.numpy as jnp
from jax import lax
from jax.experimental import pallas as pl
from jax.experimental.pallas import tpu as pltpu

F32 = jnp.float32
BF16 = jnp.bfloat16

RMS_EPS = 1e-6
LOG2E = 1.4426950408889634
SSM_WIDTH = 512
SSM_GROUP = 16
SSM_GROUPS = 32
SSM_STATE = 64
N_STATE = SSM_GROUPS * SSM_STATE
HEAD_DIM = 128
HEADS_PER_GROUP = 4
SPAN = 128
ATTN_WIDTH = HEADS_PER_GROUP * HEAD_DIM
LANES = 128
VMEM_LIMIT = 56 * 1024 * 1024

PROJ_TN = 512
D_MODEL = 1024
COL_U, COL_ZS = 0, SSM_WIDTH
COL_Q = 2 * SSM_WIDTH
COL_K = COL_Q + 3 * ATTN_WIDTH
COL_V = COL_K + 3 * ATTN_WIDTH
COL_ZA = COL_V + 3 * ATTN_WIDTH
COL_GS = COL_ZA + ATTN_WIDTH
COL_GA = COL_GS + D_MODEL
UZ_SRC = (COL_U, COL_ZS)
GATE_SRC = (COL_GS, COL_GS + PROJ_TN, COL_GA, COL_GA + PROJ_TN, COL_ZA)
GATE_WIDTH = len(GATE_SRC) * PROJ_TN
GATE_ZA_BLOCK = 4
ATTN_BATCH = 8
STATE_TILE = 2 * LANES
N_STATE_TILES = 2 * N_STATE // STATE_TILE


def _rmsnorm_f32(x, g):
    inv = lax.rsqrt(jnp.mean(x * x, axis=-1, keepdims=True) + RMS_EPS)
    return x * inv * g


def _proj_kernel(x_ref, g_ref, w_ref, o1_ref, o2_ref, o3_ref, uz_ref, gate_ref,
                 hf_ref, hn_ref, h4_ref, h16_ref, *, tm):
    d_model = x_ref.shape[-1]
    n_slab = d_model // LANES
    hf = _rmsnorm_f32(x_ref[0], g_ref[0])
    hn_ref[...] = hf.astype(BF16)
    for k in range(n_slab):
        hf_ref[k] = hf[:, k * LANES:(k + 1) * LANES]
    for r, dst in ((4, h4_ref), (16, h16_ref)):
        rows = tm // r
        for c in range(r):
            for k in range(n_slab):
                dst[c * rows:(c + 1) * rows, k * LANES:(k + 1) * LANES] = (
                    hf_ref[k, pl.ds(c, rows, stride=r), :].astype(BF16))

    def tile(h_ref, col):
        return jnp.dot(h_ref[...], w_ref[0, :, col:col + PROJ_TN], preferred_element_type=F32)

    for j, base in enumerate((COL_Q, COL_K, COL_V)):
        cols = slice(j * PROJ_TN, (j + 1) * PROJ_TN)
        o1_ref[0, 0, :, cols] = tile(hn_ref, base).astype(BF16)
        o2_ref[0, :, :, cols] = tile(h4_ref, base + ATTN_WIDTH).astype(BF16).reshape(
            4, tm // 4, PROJ_TN)
        o3_ref[0, :, :, cols] = tile(h16_ref, base + 2 * ATTN_WIDTH).astype(BF16).reshape(
            16, tm // 16, PROJ_TN)
    for jj, col in enumerate(UZ_SRC):
        uz_ref[0, :, jj * PROJ_TN:(jj + 1) * PROJ_TN] = tile(hn_ref, col)
    for jj, col in enumerate(GATE_SRC):
        gate_ref[0, :, jj * PROJ_TN:(jj + 1) * PROJ_TN] = tile(hn_ref, col).astype(gate_ref.dtype)


def _in_projection(x, g_all, w_all, layer, *, tm=512):
    bsz, seq, d_model = x.shape
    n_i = seq // tm
    kern = functools.partial(_proj_kernel, tm=tm)
    qkv_w = 3 * ATTN_WIDTH
    return pl.pallas_call(
        kern,
        grid=(bsz, n_i),
        in_specs=[
            pl.BlockSpec((1, tm, d_model), lambda b, i: (b, i, 0)),
            pl.BlockSpec((1, 1, d_model), lambda b, i: (layer, 0, 0)),
            pl.BlockSpec((1,) + w_all.shape[1:], lambda b, i: (layer, 0, 0),
                         pipeline_mode=pl.Buffered(1)),
        ],
        out_specs=[
            pl.BlockSpec((1, 1, tm, qkv_w), lambda b, i: (b, 0, i, 0)),
            pl.BlockSpec((1, 4, tm // 4, qkv_w), lambda b, i: (b, 0, i, 0)),
            pl.BlockSpec((1, 16, tm // 16, qkv_w), lambda b, i: (b, 0, i, 0)),
            pl.BlockSpec((1, tm, 2 * SSM_WIDTH), lambda b, i: (b, i, 0)),
            pl.BlockSpec((1, tm, GATE_WIDTH), lambda b, i: (b, i, 0)),
        ],
        out_shape=[
            jax.ShapeDtypeStruct((bsz, 1, seq, qkv_w), BF16),
            jax.ShapeDtypeStruct((bsz, 4, seq // 4, qkv_w), BF16),
            jax.ShapeDtypeStruct((bsz, 16, seq // 16, qkv_w), BF16),
            jax.ShapeDtypeStruct((bsz, seq, 2 * SSM_WIDTH), F32),
            jax.ShapeDtypeStruct((bsz, seq, GATE_WIDTH), BF16),
        ],
        scratch_shapes=[
            pltpu.VMEM((d_model // LANES, tm, LANES), F32),
            pltpu.VMEM((tm, d_model), BF16),
            pltpu.VMEM((tm, d_model), BF16),
            pltpu.VMEM((tm, d_model), BF16),
        ],
        compiler_params=pltpu.CompilerParams(
            dimension_semantics=("arbitrary", "arbitrary"),
            vmem_limit_bytes=VMEM_LIMIT),
        name="in_projection",
    )(x, g_all, w_all)


def _s5_kernel(uz_ref, bmat_ref, cmat_ref, lre_ref, lim_ref, dskip_ref, wglu_ref, bglu_ref,
               ys_ref, drive_ref, h_ref, state_ref, *, tt, bsz, tiles_per_chunk, scan_unroll):
    @pl.when(pl.program_id(0) == 0)
    def _():
        state_ref[...] = jnp.zeros_like(state_ref)

    rows = tt * bsz
    u = jnp.swapaxes(uz_ref[:, :, :SSM_WIDTH], 0, 1).reshape(rows, SSM_WIDTH)
    u_bf = u.astype(BF16)
    for n in range(N_STATE_TILES):
        blk = n // tiles_per_chunk
        drive_ref[:, n * STATE_TILE:(n + 1) * STATE_TILE] = jnp.dot(
            u_bf[:, blk * LANES:(blk + 1) * LANES], bmat_ref[n], preferred_element_type=F32)

    for n0 in range(0, N_STATE_TILES, tiles_per_chunk):
        tiles = range(n0, n0 + tiles_per_chunk)
        lam = [(jnp.broadcast_to(lre_ref[:, n * LANES:(n + 1) * LANES], (bsz, LANES)),
                jnp.broadcast_to(lim_ref[:, n * LANES:(n + 1) * LANES], (bsz, LANES))) for n in tiles]

        def step(t, carry, tiles=tiles, lam=lam):
            rows_t = pl.ds(pl.multiple_of(t * bsz, bsz), bsz)
            out = []
            for (hre, him), (lre, lim), n in zip(carry, lam, tiles):
                re_sl = slice(n * STATE_TILE, n * STATE_TILE + LANES)
                im_sl = slice(n * STATE_TILE + LANES, (n + 1) * STATE_TILE)
                nre = lre * hre - lim * him + drive_ref[rows_t, re_sl]
                nim = lre * him + lim * hre + drive_ref[rows_t, im_sl]
                h_ref[rows_t, re_sl] = nre
                h_ref[rows_t, im_sl] = nim
                out.append((nre, nim))
            return tuple(out)

        init = tuple((state_ref[:, n * STATE_TILE:n * STATE_TILE + LANES],
                      state_ref[:, n * STATE_TILE + LANES:(n + 1) * STATE_TILE]) for n in tiles)
        final = lax.fori_loop(0, tt, step, init, unroll=scan_unroll)
        for (hre, him), n in zip(final, tiles):
            state_ref[:, n * STATE_TILE:n * STATE_TILE + LANES] = hre
            state_ref[:, n * STATE_TILE + LANES:(n + 1) * STATE_TILE] = him

    chunk = tiles_per_chunk * STATE_TILE
    y = jnp.concatenate(
        [jnp.dot(h_ref[:, m * chunk:(m + 1) * chunk].astype(BF16), cmat_ref[m],
                 preferred_element_type=F32) for m in range(SSM_WIDTH // LANES)], axis=1)
    y = y + dskip_ref[...] * u
    y = jax.nn.gelu(y)
    gate = jnp.dot(y.astype(BF16), wglu_ref[0], preferred_element_type=F32) + bglu_ref[...]
    y = y * jax.nn.sigmoid(gate)
    z = uz_ref[:, :, SSM_WIDTH:]
    ys_ref[...] = jnp.swapaxes(y.reshape(tt, bsz, SSM_WIDTH), 0, 1) * (z * jax.nn.sigmoid(z))


def _s5_branch(uz, bmat, cmat, lre, lim, d_skip, w_glu_all, b_glu, layer, *, tt=64, scan_unroll=64):
    bsz, seq, _ = uz.shape
    tiles_per_chunk = LANES // (2 * SSM_GROUP)
    kern = functools.partial(_s5_kernel, tt=tt, bsz=bsz, tiles_per_chunk=tiles_per_chunk,
                             scan_unroll=scan_unroll)
    full = lambda shape: pl.BlockSpec(shape, lambda i: (0,) * len(shape))
    return pl.pallas_call(
        kern,
        grid=(seq // tt,),
        in_specs=[
            pl.BlockSpec((bsz, tt, 2 * SSM_WIDTH), lambda i: (0, i, 0)),
            full(bmat.shape), full(cmat.shape), full(lre.shape), full(lim.shape),
            full(d_skip.shape),
            pl.BlockSpec((1,) + w_glu_all.shape[1:], lambda i: (layer, 0, 0)),
            full(b_glu.shape),
        ],
        out_specs=pl.BlockSpec((bsz, tt, SSM_WIDTH), lambda i: (0, i, 0)),
        out_shape=jax.ShapeDtypeStruct((bsz, seq, SSM_WIDTH), F32),
        scratch_shapes=[
            pltpu.VMEM((tt * bsz, 2 * N_STATE), F32),
            pltpu.VMEM((tt * bsz, 2 * N_STATE), F32),
            pltpu.VMEM((bsz, 2 * N_STATE), F32),
        ],
        compiler_params=pltpu.CompilerParams(
            dimension_semantics=("arbitrary",), vmem_limit_bytes=VMEM_LIMIT),
        name="s5_branch",
    )(uz, bmat, cmat, lre, lim, d_skip, w_glu_all, b_glu)


def _attn_blocks(q, k, v, first):
    s = jnp.einsum('gqd,gkd->gqk', q, k, preferred_element_type=F32) * (HEAD_DIM ** -0.5 * LOG2E)
    qi = lax.broadcasted_iota(jnp.int32, s.shape[1:], 0)
    ki = lax.broadcasted_iota(jnp.int32, s.shape[1:], 1)
    valid = (ki <= qi) if first else ((ki >= qi) & (ki <= qi + SPAN))
    s = jnp.where(valid[None], s, -jnp.inf)
    m = jnp.max(s, axis=-1, keepdims=True)
    p = jnp.exp2(s - m)
    l = jnp.sum(p, axis=-1, keepdims=True)
    acc = jnp.einsum('gqk,gkd->gqd', p.astype(BF16), v, preferred_element_type=F32)
    return acc / l, m + jnp.log2(l)


def _attn_kernel(q1_ref, k1_ref, v1_ref, q2_ref, k2_ref, v2_ref, q3_ref, k3_ref, v3_ref,
                 y_ref, out_ref, lse_ref):
    groups = ((q1_ref, k1_ref, v1_ref), (q2_ref, k2_ref, v2_ref), (q3_ref, k3_ref, v3_ref))
    for gi, (q_ref, k_ref, v_ref) in enumerate(groups):
        r = q_ref.shape[1]
        n_blk = q_ref.shape[2] // SPAN

        def put(blocks, o, lse, gi=gi, r=r):
            for g, (c, n) in enumerate(blocks):
                if r == 1:
                    rows = pl.ds(n * SPAN, SPAN)
                else:
                    rows = pl.ds(c + r * SPAN * n, SPAN, stride=r)
                out_ref[gi, rows, :] = o[g]
                lse_ref[gi, rows, :] = jnp.broadcast_to(lse[g], o[g].shape)

        first = [(c, 0) for c in range(r)]
        rest = [(c, n) for c in range(r) for n in range(1, n_blk)]
        for blocks, is_first in ((first, True), (rest, False)):
            for lo in range(0, len(blocks), ATTN_BATCH):
                chunk = blocks[lo:lo + ATTN_BATCH]
                q = jnp.stack([q_ref[0, c, n * SPAN:(n + 1) * SPAN, :] for c, n in chunk])
                k0 = 0 if is_first else 1
                k = jnp.stack([k_ref[0, c, (n - k0) * SPAN:(n + 1) * SPAN, :] for c, n in chunk])
                v = jnp.stack([v_ref[0, c, (n - k0) * SPAN:(n + 1) * SPAN, :] for c, n in chunk])
                o, lse = _attn_blocks(q, k, v, is_first)
                put(chunk, o, lse)

    lse = lse_ref[...]
    mx = jnp.max(lse, axis=0)
    w = jnp.exp2(lse - mx[None])
    y_ref[0] = jnp.sum(w * out_ref[...], axis=0) / jnp.sum(w, axis=0)


def _attention(o1, o2, o3):
    bsz, _, seq, _ = o1.shape

    def specs(arr):
        _, r, n, _ = arr.shape
        return [pl.BlockSpec((1, r, n, HEAD_DIM), lambda b, h, part=part: (b, 0, 0, part * HEADS_PER_GROUP + h))
                for part in range(3)]

    return pl.pallas_call(
        _attn_kernel,
        grid=(bsz, HEADS_PER_GROUP),
        in_specs=specs(o1) + specs(o2) + specs(o3),
        out_specs=pl.BlockSpec((1, seq, HEAD_DIM), lambda b, h: (b, 0, h)),
        out_shape=jax.ShapeDtypeStruct((bsz, seq, ATTN_WIDTH), F32),
        scratch_shapes=[
            pltpu.VMEM((3, seq, HEAD_DIM), F32),
            pltpu.VMEM((3, seq, HEAD_DIM), F32),
        ],
        compiler_params=pltpu.CompilerParams(
            dimension_semantics=("arbitrary", "arbitrary"), vmem_limit_bytes=VMEM_LIMIT),
        name="dilated_attention",
    )(o1, o1, o1, o2, o2, o2, o3, o3, o3)


def _merge_kernel(x_ref, ys_ref, ya_ref, za_ref, gs_ref, ga_ref, wbs_ref, wba_ref, wout_ref, pg_ref,
                  o_ref):
    za = za_ref[0].astype(F32)
    ya = ya_ref[0] * (za * jax.nn.sigmoid(za))
    ms = jnp.dot(ys_ref[0].astype(BF16), wbs_ref[0], preferred_element_type=F32)
    ma = jnp.dot(ya.astype(BF16), wba_ref[0], preferred_element_type=F32)
    merged = (jax.nn.sigmoid(gs_ref[0].astype(F32)) * ms
              + jax.nn.sigmoid(ga_ref[0].astype(F32)) * ma)
    out = jnp.dot(merged.astype(BF16), wout_ref[0], preferred_element_type=F32)
    o_ref[0] = x_ref[0] + _rmsnorm_f32(out, pg_ref[0])


def _merge(x, ys, ya, gates, wbs_all, wba_all, wout_all, pg_all, layer, *, tm=512):
    bsz, seq, d_model = x.shape
    row = lambda width, blk: pl.BlockSpec((1, tm, width), lambda b, i: (b, i, blk))
    per_layer = lambda arr: pl.BlockSpec((1,) + arr.shape[1:], lambda b, i: (layer, 0, 0))
    return pl.pallas_call(
        _merge_kernel,
        grid=(bsz, seq // tm),
        in_specs=[
            row(d_model, 0), row(SSM_WIDTH, 0), row(ATTN_WIDTH, 0),
            row(ATTN_WIDTH, GATE_ZA_BLOCK),
            row(d_model, 0),
            row(d_model, 1),
            per_layer(wbs_all), per_layer(wba_all), per_layer(wout_all), per_layer(pg_all),
        ],
        out_specs=row(d_model, 0),
        out_shape=jax.ShapeDtypeStruct((bsz, seq, d_model), F32),
        compiler_params=pltpu.CompilerParams(
            dimension_semantics=("arbitrary", "arbitrary"), vmem_limit_bytes=VMEM_LIMIT),
        name="merge_out",
    )(x, ys, ya, gates, gates, gates, wbs_all, wba_all, wout_all, pg_all)


def _ssm_params(lam_re, lam_im, log_dt, b_re, b_im, c_re, c_im):
    depth = lam_re.shape[0]
    a = jnp.minimum(lam_re, -1e-4)
    b = lam_im
    dt = jnp.exp(log_dt)[..., None]
    mag = jnp.exp(a * dt)
    lbr = mag * jnp.cos(b * dt)
    lbi = mag * jnp.sin(b * dt)
    nr, ni = lbr - 1.0, lbi
    den = a * a + b * b
    kr = (nr * a + ni * b) / den
    ki = (ni * a - nr * b) / den
    bbr = kr[..., None] * b_re - ki[..., None] * b_im
    bbi = kr[..., None] * b_im + ki[..., None] * b_re
    per_block = LANES // (2 * SSM_GROUP)
    eye2 = jnp.eye(2, dtype=F32)
    slot = jax.nn.one_hot(jnp.arange(N_STATE_TILES) % per_block, per_block, dtype=F32)
    b5 = jnp.stack([bbr, bbi], axis=1).reshape(depth, 2, N_STATE_TILES, 2, SSM_STATE, SSM_GROUP)
    bblk = jnp.einsum('langpc,gh->lngcahp', b5, eye2).reshape(
        depth, N_STATE_TILES, 2 * SSM_GROUP, STATE_TILE)
    bmat = (bblk[:, :, None] * slot[None, :, :, None, None]).reshape(
        depth, N_STATE_TILES, LANES, STATE_TILE).astype(BF16)
    c5 = jnp.stack([c_re, -c_im], axis=1).reshape(depth, 2, N_STATE_TILES, 2, SSM_GROUP, SSM_STATE)
    cblk = jnp.einsum('langcp,gh->lnagphc', c5, eye2).reshape(
        depth, N_STATE_TILES, STATE_TILE, 2 * SSM_GROUP)
    cmat = (cblk[:, :, :, None, :] * slot[None, :, None, :, None]).reshape(
        depth, N_STATE_TILES // per_block, per_block * STATE_TILE, LANES).astype(BF16)
    return bmat, cmat, lbr.reshape(depth, 1, N_STATE), lbi.reshape(depth, 1, N_STATE)


def kernel(x, pre_norm_g, w_in, lambda_re, lambda_im, log_dt, b_re, b_im, c_re, c_im, d_skip, w_glu, b_glu, w_branch_s, w_branch_a, w_out, post_norm_g):
    depth = w_in.shape[0]
    assert x.shape[-1] == D_MODEL and w_in.shape[-1] == COL_GA + D_MODEL
    w_in_bf = w_in.astype(BF16)
    w_glu_bf = w_glu.astype(BF16)
    wbs_bf, wba_bf, wout_bf = (w.astype(BF16) for w in (w_branch_s, w_branch_a, w_out))
    pre_g = pre_norm_g[:, None, :]
    post_g = post_norm_g[:, None, :]
    bmat, cmat, lre, lim = _ssm_params(lambda_re, lambda_im, log_dt, b_re, b_im, c_re, c_im)
    for l in range(depth):
        o1, o2, o3, uz, gates = _in_projection(x, pre_g, w_in_bf, l)
        ys = _s5_branch(uz, bmat[l], cmat[l], lre[l], lim[l], d_skip[l][None, :], w_glu_bf,
                        b_glu[l][None, :], l)
        ya = _attention(o1, o2, o3)
        x = _merge(x, ys, ya, gates, wbs_bf, wba_bf, wout_bf, post_g, l)
    return x
```

```python
import functools

import jax
import jax.numpy as jnp
from jax import lax
from jax.experimental import pallas as pl
from jax.experimental.pallas import tpu as pltpu

F32 = jnp.float32
BF16 = jnp.bfloat16

RMS_EPS = 1e-6
LOG2E = 1.4426950408889634
SSM_WIDTH = 512
SSM_GROUP = 16
SSM_GROUPS = 32
SSM_STATE = 64
N_STATE = SSM_GROUPS * SSM_STATE
HEAD_DIM = 128
HEADS_PER_GROUP = 4
SPAN = 128
ATTN_WIDTH = HEADS_PER_GROUP * HEAD_DIM
LANES = 128
VMEM_LIMIT = 56 * 1024 * 1024

PROJ_TN = 512
D_MODEL = 1024
COL_U, COL_ZS = 0, SSM_WIDTH
COL_Q = 2 * SSM_WIDTH
COL_K = COL_Q + 3 * ATTN_WIDTH
COL_V = COL_K + 3 * ATTN_WIDTH
COL_ZA = COL_V + 3 * ATTN_WIDTH
COL_GS = COL_ZA + ATTN_WIDTH
COL_GA = COL_GS + D_MODEL
UZ_SRC = (COL_U, COL_ZS)
GATE_SRC = (COL_GS, COL_GS + PROJ_TN, COL_GA, COL_GA + PROJ_TN, COL_ZA)
GATE_WIDTH = len(GATE_SRC) * PROJ_TN
GATE_ZA_BLOCK = 4
ATTN_BATCH = 16
STATE_TILE = 2 * LANES
N_STATE_TILES = 2 * N_STATE // STATE_TILE


def _rmsnorm_f32(x, g):
    inv = lax.rsqrt(jnp.mean(x * x, axis=-1, keepdims=True) + RMS_EPS)
    return x * inv * g


def _proj_kernel(x_ref, g_ref, w_ref, o1_ref, o2_ref, o3_ref, uz_ref, gate_ref,
                 hf_ref, hn_ref, h4_ref, h16_ref, *, tm):
    d_model = x_ref.shape[-1]
    n_slab = d_model // LANES
    hf = _rmsnorm_f32(x_ref[0], g_ref[0])
    hn_ref[...] = hf.astype(BF16)
    for k in range(n_slab):
        hf_ref[k] = hf[:, k * LANES:(k + 1) * LANES]
    for r, dst in ((4, h4_ref), (16, h16_ref)):
        rows = tm // r
        for c in range(r):
            for k in range(n_slab):
                dst[c * rows:(c + 1) * rows, k * LANES:(k + 1) * LANES] = (
                    hf_ref[k, pl.ds(c, rows, stride=r), :].astype(BF16))

    def tile(h_ref, col):
        return jnp.dot(h_ref[...], w_ref[0, :, col:col + PROJ_TN], preferred_element_type=F32)

    for j, base in enumerate((COL_Q, COL_K, COL_V)):
        cols = slice(j * PROJ_TN, (j + 1) * PROJ_TN)
        o1_ref[0, 0, :, cols] = tile(hn_ref, base).astype(BF16)
        o2_ref[0, :, :, cols] = tile(h4_ref, base + ATTN_WIDTH).astype(BF16).reshape(
            4, tm // 4, PROJ_TN)
        o3_ref[0, :, :, cols] = tile(h16_ref, base + 2 * ATTN_WIDTH).astype(BF16).reshape(
            16, tm // 16, PROJ_TN)
    for jj, col in enumerate(UZ_SRC):
        uz_ref[0, :, jj * PROJ_TN:(jj + 1) * PROJ_TN] = tile(hn_ref, col)
    for jj, col in enumerate(GATE_SRC):
        gate_ref[0, :, jj * PROJ_TN:(jj + 1) * PROJ_TN] = tile(hn_ref, col).astype(gate_ref.dtype)


def _in_projection(x, g_all, w_all, layer, *, tm=512):
    bsz, seq, d_model = x.shape
    n_i = seq // tm
    kern = functools.partial(_proj_kernel, tm=tm)
    qkv_w = 3 * ATTN_WIDTH
    return pl.pallas_call(
        kern,
        grid=(bsz, n_i),
        in_specs=[
            pl.BlockSpec((1, tm, d_model), lambda b, i: (b, i, 0)),
            pl.BlockSpec((1, 1, d_model), lambda b, i: (layer, 0, 0)),
            pl.BlockSpec((1,) + w_all.shape[1:], lambda b, i: (layer, 0, 0),
                         pipeline_mode=pl.Buffered(1)),
        ],
        out_specs=[
            pl.BlockSpec((1, 1, tm, qkv_w), lambda b, i: (b, 0, i, 0)),
            pl.BlockSpec((1, 4, tm // 4, qkv_w), lambda b, i: (b, 0, i, 0)),
            pl.BlockSpec((1, 16, tm // 16, qkv_w), lambda b, i: (b, 0, i, 0)),
            pl.BlockSpec((1, tm, 2 * SSM_WIDTH), lambda b, i: (b, i, 0)),
            pl.BlockSpec((1, tm, GATE_WIDTH), lambda b, i: (b, i, 0)),
        ],
        out_shape=[
            jax.ShapeDtypeStruct((bsz, 1, seq, qkv_w), BF16),
            jax.ShapeDtypeStruct((bsz, 4, seq // 4, qkv_w), BF16),
            jax.ShapeDtypeStruct((bsz, 16, seq // 16, qkv_w), BF16),
            jax.ShapeDtypeStruct((bsz, seq, 2 * SSM_WIDTH), F32),
            jax.ShapeDtypeStruct((bsz, seq, GATE_WIDTH), BF16),
        ],
        scratch_shapes=[
            pltpu.VMEM((d_model // LANES, tm, LANES), F32),
            pltpu.VMEM((tm, d_model), BF16),
            pltpu.VMEM((tm, d_model), BF16),
            pltpu.VMEM((tm, d_model), BF16),
        ],
        compiler_params=pltpu.CompilerParams(
            dimension_semantics=("arbitrary", "arbitrary"),
            vmem_limit_bytes=VMEM_LIMIT),
        name="in_projection",
    )(x, g_all, w_all)


def _s5_kernel(uz_ref, bmat_ref, cmat_ref, lre_ref, lim_ref, dskip_ref, wglu_ref, bglu_ref,
               ys_ref, drive_ref, h_ref, state_ref, *, tt, bsz, tiles_per_chunk, scan_unroll):
    @pl.when(pl.program_id(0) == 0)
    def _():
        state_ref[...] = jnp.zeros_like(state_ref)

    rows = tt * bsz
    u = jnp.swapaxes(uz_ref[:, :, :SSM_WIDTH], 0, 1).reshape(rows, SSM_WIDTH)
    u_bf = u.astype(BF16)
    for n in range(N_STATE_TILES):
        blk = n // tiles_per_chunk
        drive_ref[:, n * STATE_TILE:(n + 1) * STATE_TILE] = jnp.dot(
            u_bf[:, blk * LANES:(blk + 1) * LANES], bmat_ref[n], preferred_element_type=F32)

    for n0 in range(0, N_STATE_TILES, tiles_per_chunk):
        tiles = range(n0, n0 + tiles_per_chunk)
        lam = [(jnp.broadcast_to(lre_ref[:, n * LANES:(n + 1) * LANES], (bsz, LANES)),
                jnp.broadcast_to(lim_ref[:, n * LANES:(n + 1) * LANES], (bsz, LANES))) for n in tiles]

        def step(t, carry, tiles=tiles, lam=lam):
            rows_t = pl.ds(pl.multiple_of(t * bsz, bsz), bsz)
            out = []
            for (hre, him), (lre, lim), n in zip(carry, lam, tiles):
                re_sl = slice(n * STATE_TILE, n * STATE_TILE + LANES)
                im_sl = slice(n * STATE_TILE + LANES, (n + 1) * STATE_TILE)
                nre = lre * hre - lim * him + drive_ref[rows_t, re_sl]
                nim = lre * him + lim * hre + drive_ref[rows_t, im_sl]
                h_ref[rows_t, re_sl] = nre
                h_ref[rows_t, im_sl] = nim
                out.append((nre, nim))
            return tuple(out)

        init = tuple((state_ref[:, n * STATE_TILE:n * STATE_TILE + LANES],
                      state_ref[:, n * STATE_TILE + LANES:(n + 1) * STATE_TILE]) for n in tiles)
        final = lax.fori_loop(0, tt, step, init, unroll=scan_unroll)
        for (hre, him), n in zip(final, tiles):
            state_ref[:, n * STATE_TILE:n * STATE_TILE + LANES] = hre
            state_ref[:, n * STATE_TILE + LANES:(n + 1) * STATE_TILE] = him

    chunk = tiles_per_chunk * STATE_TILE
    y = jnp.concatenate(
        [jnp.dot(h_ref[:, m * chunk:(m + 1) * chunk].astype(BF16), cmat_ref[m],
                 preferred_element_type=F32) for m in range(SSM_WIDTH // LANES)], axis=1)
    y = y + dskip_ref[...] * u
    y = jax.nn.gelu(y)
    gate = jnp.dot(y.astype(BF16), wglu_ref[0], preferred_element_type=F32) + bglu_ref[...]
    y = y * jax.nn.sigmoid(gate)
    z = uz_ref[:, :, SSM_WIDTH:]
    ys = jnp.swapaxes(y.reshape(tt, bsz, SSM_WIDTH), 0, 1) * (z * jax.nn.sigmoid(z))
    ys_ref[...] = ys.astype(ys_ref.dtype)


def _s5_branch(uz, bmat, cmat, lre, lim, d_skip, w_glu_all, b_glu, layer, *, tt=128, scan_unroll=128):
    bsz, seq, _ = uz.shape
    tiles_per_chunk = LANES // (2 * SSM_GROUP)
    kern = functools.partial(_s5_kernel, tt=tt, bsz=bsz, tiles_per_chunk=tiles_per_chunk,
                             scan_unroll=scan_unroll)
    full = lambda shape: pl.BlockSpec(shape, lambda i: (0,) * len(shape))
    return pl.pallas_call(
        kern,
        grid=(seq // tt,),
        in_specs=[
            pl.BlockSpec((bsz, tt, 2 * SSM_WIDTH), lambda i: (0, i, 0)),
            full(bmat.shape), full(cmat.shape), full(lre.shape), full(lim.shape),
            full(d_skip.shape),
            pl.BlockSpec((1,) + w_glu_all.shape[1:], lambda i: (layer, 0, 0)),
            full(b_glu.shape),
        ],
        out_specs=pl.BlockSpec((bsz, tt, SSM_WIDTH), lambda i: (0, i, 0)),
        out_shape=jax.ShapeDtypeStruct((bsz, seq, SSM_WIDTH), BF16),
        scratch_shapes=[
            pltpu.VMEM((tt * bsz, 2 * N_STATE), F32),
            pltpu.VMEM((tt * bsz, 2 * N_STATE), F32),
            pltpu.VMEM((bsz, 2 * N_STATE), F32),
        ],
        compiler_params=pltpu.CompilerParams(
            dimension_semantics=("arbitrary",), vmem_limit_bytes=VMEM_LIMIT),
        name="s5_branch",
    )(uz, bmat, cmat, lre, lim, d_skip, w_glu_all, b_glu)


def _attn_blocks(q, k, v, first):
    s = jnp.einsum('gqd,gkd->gqk', q, k, preferred_element_type=F32) * (HEAD_DIM ** -0.5 * LOG2E)
    qi = lax.broadcasted_iota(jnp.int32, s.shape[1:], 0)
    ki = lax.broadcasted_iota(jnp.int32, s.shape[1:], 1)
    valid = (ki <= qi) if first else ((ki >= qi) & (ki <= qi + SPAN))
    s = jnp.where(valid[None], s, -jnp.inf)
    m = jnp.max(s, axis=-1, keepdims=True)
    p = jnp.exp2(s - m)
    l = jnp.sum(p, axis=-1, keepdims=True)
    acc = jnp.einsum('gqk,gkd->gqd', p.astype(BF16), v, preferred_element_type=F32)
    return acc / l, m + jnp.log2(l)


def _attn_kernel(q1_ref, k1_ref, v1_ref, q2_ref, k2_ref, v2_ref, q3_ref, k3_ref, v3_ref,
                 za_ref, y_ref, out_ref, lse_ref):
    groups = ((q1_ref, k1_ref, v1_ref), (q2_ref, k2_ref, v2_ref), (q3_ref, k3_ref, v3_ref))
    for gi, (q_ref, k_ref, v_ref) in enumerate(groups):
        r = q_ref.shape[1]
        n_blk = q_ref.shape[2] // SPAN

        def put(blocks, o, lse, gi=gi, r=r):
            for g, (c, n) in enumerate(blocks):
                if r == 1:
                    rows = pl.ds(n * SPAN, SPAN)
                else:
                    rows = pl.ds(c + r * SPAN * n, SPAN, stride=r)
                out_ref[gi, rows, :] = o[g]
                lse_ref[gi, rows, :] = jnp.broadcast_to(lse[g], o[g].shape)

        first = [(c, 0) for c in range(r)]
        rest = [(c, n) for c in range(r) for n in range(1, n_blk)]
        for blocks, is_first in ((first, True), (rest, False)):
            for lo in range(0, len(blocks), ATTN_BATCH):
                chunk = blocks[lo:lo + ATTN_BATCH]
                q = jnp.stack([q_ref[0, c, n * SPAN:(n + 1) * SPAN, :] for c, n in chunk])
                k0 = 0 if is_first else 1
                k = jnp.stack([k_ref[0, c, (n - k0) * SPAN:(n + 1) * SPAN, :] for c, n in chunk])
                v = jnp.stack([v_ref[0, c, (n - k0) * SPAN:(n + 1) * SPAN, :] for c, n in chunk])
                o, lse = _attn_blocks(q, k, v, is_first)
                put(chunk, o, lse)

    lse = lse_ref[...]
    mx = jnp.max(lse, axis=0)
    w = jnp.exp2(lse - mx[None])
    y = jnp.sum(w * out_ref[...], axis=0) / jnp.sum(w, axis=0)
    za = za_ref[0].astype(F32)
    y_ref[0] = (y * (za * jax.nn.sigmoid(za))).astype(y_ref.dtype)


def _attention(o1, o2, o3, gates):
    bsz, _, seq, _ = o1.shape

    def specs(arr):
        _, r, n, _ = arr.shape
        return [pl.BlockSpec((1, r, n, HEAD_DIM), lambda b, h, part=part: (b, 0, 0, part * HEADS_PER_GROUP + h))
                for part in range(3)]

    return pl.pallas_call(
        _attn_kernel,
        grid=(bsz, HEADS_PER_GROUP),
        in_specs=specs(o1) + specs(o2) + specs(o3) + [
            pl.BlockSpec((1, seq, HEAD_DIM),
                         lambda b, h: (b, 0, GATE_ZA_BLOCK * PROJ_TN // HEAD_DIM + h))],
        out_specs=pl.BlockSpec((1, seq, HEAD_DIM), lambda b, h: (b, 0, h)),
        out_shape=jax.ShapeDtypeStruct((bsz, seq, ATTN_WIDTH), BF16),
        scratch_shapes=[
            pltpu.VMEM((3, seq, HEAD_DIM), F32),
            pltpu.VMEM((3, seq, HEAD_DIM), F32),
        ],
        compiler_params=pltpu.CompilerParams(
            dimension_semantics=("arbitrary", "arbitrary"), vmem_limit_bytes=VMEM_LIMIT),
        name="dilated_attention",
    )(o1, o1, o1, o2, o2, o2, o3, o3, o3, gates)


def _merge_kernel(x_ref, ys_ref, ya_ref, gs_ref, ga_ref, wbs_ref, wba_ref, wout_ref, pg_ref, o_ref):
    ms = jnp.dot(ys_ref[0], wbs_ref[0], preferred_element_type=F32)
    ma = jnp.dot(ya_ref[0], wba_ref[0], preferred_element_type=F32)
    merged = (jax.nn.sigmoid(gs_ref[0].astype(F32)) * ms
              + jax.nn.sigmoid(ga_ref[0].astype(F32)) * ma)
    out = jnp.dot(merged.astype(BF16), wout_ref[0], preferred_element_type=F32)
    o_ref[0] = x_ref[0] + _rmsnorm_f32(out, pg_ref[0])


def _merge(x, ys, ya, gates, wbs_all, wba_all, wout_all, pg_all, layer, *, tm=512):
    bsz, seq, d_model = x.shape
    row = lambda width, blk: pl.BlockSpec((1, tm, width), lambda b, i: (b, i, blk))
    per_layer = lambda arr: pl.BlockSpec((1,) + arr.shape[1:], lambda b, i: (layer, 0, 0))
    return pl.pallas_call(
        _merge_kernel,
        grid=(bsz, seq // tm),
        in_specs=[
            row(d_model, 0), row(SSM_WIDTH, 0), row(ATTN_WIDTH, 0),
            row(d_model, 0),
            row(d_model, 1),
            per_layer(wbs_all), per_layer(wba_all), per_layer(wout_all), per_layer(pg_all),
        ],
        out_specs=row(d_model, 0),
        out_shape=jax.ShapeDtypeStruct((bsz, seq, d_model), F32),
        compiler_params=pltpu.CompilerParams(
            dimension_semantics=("arbitrary", "arbitrary"), vmem_limit_bytes=VMEM_LIMIT),
        name="merge_out",
    )(x, ys, ya, gates, gates, wbs_all, wba_all, wout_all, pg_all)


def _ssm_params(lam_re, lam_im, log_dt, b_re, b_im, c_re, c_im):
    depth = lam_re.shape[0]
    a = jnp.minimum(lam_re, -1e-4)
    b = lam_im
    dt = jnp.exp(log_dt)[..., None]
    mag = jnp.exp(a * dt)
    lbr = mag * jnp.cos(b * dt)
    lbi = mag * jnp.sin(b * dt)
    nr, ni = lbr - 1.0, lbi
    den = a * a + b * b
    kr = (nr * a + ni * b) / den
    ki = (ni * a - nr * b) / den
    bbr = kr[..., None] * b_re - ki[..., None] * b_im
    bbi = kr[..., None] * b_im + ki[..., None] * b_re
    per_block = LANES // (2 * SSM_GROUP)
    eye2 = jnp.eye(2, dtype=F32)
    slot = jax.nn.one_hot(jnp.arange(N_STATE_TILES) % per_block, per_block, dtype=F32)
    b5 = jnp.stack([bbr, bbi], axis=1).reshape(depth, 2, N_STATE_TILES, 2, SSM_STATE, SSM_GROUP)
    bblk = jnp.einsum('langpc,gh->lngcahp', b5, eye2).reshape(
        depth, N_STATE_TILES, 2 * SSM_GROUP, STATE_TILE)
    bmat = (bblk[:, :, None] * slot[None, :, :, None, None]).reshape(
        depth, N_STATE_TILES, LANES, STATE_TILE).astype(BF16)
    c5 = jnp.stack([c_re, -c_im], axis=1).reshape(depth, 2, N_STATE_TILES, 2, SSM_GROUP, SSM_STATE)
    cblk = jnp.einsum('langcp,gh->lnagphc', c5, eye2).reshape(
        depth, N_STATE_TILES, STATE_TILE, 2 * SSM_GROUP)
    cmat = (cblk[:, :, :, None, :] * slot[None, :, None, :, None]).reshape(
        depth, N_STATE_TILES // per_block, per_block * STATE_TILE, LANES).astype(BF16)
    return bmat, cmat, lbr.reshape(depth, 1, N_STATE), lbi.reshape(depth, 1, N_STATE)


def kernel(x, pre_norm_g, w_in, lambda_re, lambda_im, log_dt, b_re, b_im, c_re, c_im, d_skip, w_glu, b_glu, w_branch_s, w_branch_a, w_out, post_norm_g):
    depth = w_in.shape[0]
    assert x.shape[-1] == D_MODEL and w_in.shape[-1] == COL_GA + D_MODEL
    w_in_bf = w_in.astype(BF16)
    w_glu_bf = w_glu.astype(BF16)
    wbs_bf, wba_bf, wout_bf = (w.astype(BF16) for w in (w_branch_s, w_branch_a, w_out))
    pre_g = pre_norm_g[:, None, :]
    post_g = post_norm_g[:, None, :]
    bmat, cmat, lre, lim = _ssm_params(lambda_re, lambda_im, log_dt, b_re, b_im, c_re, c_im)
    for l in range(depth):
        o1, o2, o3, uz, gates = _in_projection(x, pre_g, w_in_bf, l)
        ys = _s5_branch(uz, bmat[l], cmat[l], lre[l], lim[l], d_skip[l][None, :], w_glu_bf,
                        b_glu[l][None, :], l)
        ya = _attention(o1, o2, o3, gates)
        x = _merge(x, ys, ya, gates, wbs_bf, wba_bf, wout_bf, post_g, l)
    return x
```

```python
import functools

import jax
import jax.numpy as jnp
from jax import lax
from jax.experimental import pallas as pl
from jax.experimental.pallas import tpu as pltpu

F32 = jnp.float32
BF16 = jnp.bfloat16

RMS_EPS = 1e-6
LOG2E = 1.4426950408889634
SSM_WIDTH = 512
SSM_GROUP = 16
SSM_GROUPS = 32
SSM_STATE = 64
N_STATE = SSM_GROUPS * SSM_STATE
HEAD_DIM = 128
HEADS_PER_GROUP = 4
SPAN = 128
ATTN_WIDTH = HEADS_PER_GROUP * HEAD_DIM
LANES = 128
VMEM_LIMIT = 56 * 1024 * 1024

PROJ_TN = 512
D_MODEL = 1024
COL_U, COL_ZS = 0, SSM_WIDTH
COL_Q = 2 * SSM_WIDTH
COL_K = COL_Q + 3 * ATTN_WIDTH
COL_V = COL_K + 3 * ATTN_WIDTH
COL_ZA = COL_V + 3 * ATTN_WIDTH
COL_GS = COL_ZA + ATTN_WIDTH
COL_GA = COL_GS + D_MODEL
UZ_SRC = (COL_U, COL_ZS)
GATE_SRC = (COL_GS, COL_GS + PROJ_TN, COL_GA, COL_GA + PROJ_TN)
GATE_WIDTH = len(GATE_SRC) * PROJ_TN
QKV_SLOTS = 3 * HEADS_PER_GROUP
ATTN_BATCH = 16
STATE_TILE = 2 * LANES
N_STATE_TILES = 2 * N_STATE // STATE_TILE


def _rmsnorm_f32(x, g):
    inv = lax.rsqrt(jnp.mean(x * x, axis=-1, keepdims=True) + RMS_EPS)
    return x * inv * g


def _proj_kernel(x_ref, g_ref, w_ref, o1_ref, o2_ref, o3_ref, uz_ref, gate_ref, za_ref,
                 hf_ref, hn_ref, h4_ref, h16_ref, *, tm):
    d_model = x_ref.shape[-1]
    n_slab = d_model // LANES
    hf = _rmsnorm_f32(x_ref[0], g_ref[0])
    hn_ref[...] = hf.astype(BF16)
    for k in range(n_slab):
        hf_ref[k] = hf[:, k * LANES:(k + 1) * LANES]
    for r, dst in ((4, h4_ref), (16, h16_ref)):
        rows = tm // r
        for c in range(r):
            for k in range(n_slab):
                dst[c * rows:(c + 1) * rows, k * LANES:(k + 1) * LANES] = (
                    hf_ref[k, pl.ds(c, rows, stride=r), :].astype(BF16))

    def tile(h_ref, col):
        return jnp.dot(h_ref[...], w_ref[0, :, col:col + PROJ_TN], preferred_element_type=F32)

    def heads(t):
        return [t[:, hh * HEAD_DIM:(hh + 1) * HEAD_DIM] for hh in range(HEADS_PER_GROUP)]

    for j, base in enumerate((COL_Q, COL_K, COL_V)):
        for gi, (o_ref, h_ref, r) in enumerate(((o1_ref, hn_ref, 1), (o2_ref, h4_ref, 4),
                                               (o3_ref, h16_ref, 16))):
            t = tile(h_ref, base + gi * ATTN_WIDTH).astype(BF16)
            for hh, th in enumerate(heads(t)):
                o_ref[0, :, j * HEADS_PER_GROUP + hh, :, :] = th.reshape(r, tm // r, HEAD_DIM)
    for hh, th in enumerate(heads(tile(hn_ref, COL_ZA).astype(BF16))):
        za_ref[0, hh] = th
    for jj, col in enumerate(UZ_SRC):
        uz_ref[0, :, jj * PROJ_TN:(jj + 1) * PROJ_TN] = tile(hn_ref, col)
    for jj, col in enumerate(GATE_SRC):
        gate_ref[0, :, jj * PROJ_TN:(jj + 1) * PROJ_TN] = tile(hn_ref, col).astype(gate_ref.dtype)


def _in_projection(x, g_all, w_all, layer, *, tm=512):
    bsz, seq, d_model = x.shape
    n_i = seq // tm
    kern = functools.partial(_proj_kernel, tm=tm)
    qkv = lambda r: (1, r, QKV_SLOTS, tm // r, HEAD_DIM)
    qkv_all = lambda r: (bsz, r, QKV_SLOTS, seq // r, HEAD_DIM)
    return pl.pallas_call(
        kern,
        grid=(bsz, n_i),
        in_specs=[
            pl.BlockSpec((1, tm, d_model), lambda b, i: (b, i, 0)),
            pl.BlockSpec((1, 1, d_model), lambda b, i: (layer, 0, 0)),
            pl.BlockSpec((1,) + w_all.shape[1:], lambda b, i: (layer, 0, 0),
                         pipeline_mode=pl.Buffered(1)),
        ],
        out_specs=[
            pl.BlockSpec(qkv(1), lambda b, i: (b, 0, 0, i, 0)),
            pl.BlockSpec(qkv(4), lambda b, i: (b, 0, 0, i, 0)),
            pl.BlockSpec(qkv(16), lambda b, i: (b, 0, 0, i, 0)),
            pl.BlockSpec((1, tm, 2 * SSM_WIDTH), lambda b, i: (b, i, 0)),
            pl.BlockSpec((1, tm, GATE_WIDTH), lambda b, i: (b, i, 0)),
            pl.BlockSpec((1, HEADS_PER_GROUP, tm, HEAD_DIM), lambda b, i: (b, 0, i, 0)),
        ],
        out_shape=[
            jax.ShapeDtypeStruct(qkv_all(1), BF16),
            jax.ShapeDtypeStruct(qkv_all(4), BF16),
            jax.ShapeDtypeStruct(qkv_all(16), BF16),
            jax.ShapeDtypeStruct((bsz, seq, 2 * SSM_WIDTH), F32),
            jax.ShapeDtypeStruct((bsz, seq, GATE_WIDTH), BF16),
            jax.ShapeDtypeStruct((bsz, HEADS_PER_GROUP, seq, HEAD_DIM), BF16),
        ],
        scratch_shapes=[
            pltpu.VMEM((d_model // LANES, tm, LANES), F32),
            pltpu.VMEM((tm, d_model), BF16),
            pltpu.VMEM((tm, d_model), BF16),
            pltpu.VMEM((tm, d_model), BF16),
        ],
        compiler_params=pltpu.CompilerParams(
            dimension_semantics=("arbitrary", "arbitrary"),
            vmem_limit_bytes=VMEM_LIMIT),
        name="in_projection",
    )(x, g_all, w_all)


def _s5_kernel(uz_ref, bmat_ref, cmat_ref, lre_ref, lim_ref, dskip_ref, wglu_ref, bglu_ref,
               ys_ref, drive_ref, h_ref, state_ref, *, tt, bsz, tiles_per_chunk, scan_unroll):
    @pl.when(pl.program_id(0) == 0)
    def _():
        state_ref[...] = jnp.zeros_like(state_ref)

    rows = tt * bsz
    u = jnp.swapaxes(uz_ref[:, :, :SSM_WIDTH], 0, 1).reshape(rows, SSM_WIDTH)
    u_bf = u.astype(BF16)
    for n in range(N_STATE_TILES):
        blk = n // tiles_per_chunk
        drive_ref[:, n * STATE_TILE:(n + 1) * STATE_TILE] = jnp.dot(
            u_bf[:, blk * LANES:(blk + 1) * LANES], bmat_ref[n], preferred_element_type=F32)

    for n0 in range(0, N_STATE_TILES, tiles_per_chunk):
        tiles = range(n0, n0 + tiles_per_chunk)
        lam = [(jnp.broadcast_to(lre_ref[:, n * LANES:(n + 1) * LANES], (bsz, LANES)),
                jnp.broadcast_to(lim_ref[:, n * LANES:(n + 1) * LANES], (bsz, LANES))) for n in tiles]

        def step(t, carry, tiles=tiles, lam=lam):
            rows_t = pl.ds(pl.multiple_of(t * bsz, bsz), bsz)
            out = []
            for (hre, him), (lre, lim), n in zip(carry, lam, tiles):
                re_sl = slice(n * STATE_TILE, n * STATE_TILE + LANES)
                im_sl = slice(n * STATE_TILE + LANES, (n + 1) * STATE_TILE)
                nre = lre * hre - lim * him + drive_ref[rows_t, re_sl]
                nim = lre * him + lim * hre + drive_ref[rows_t, im_sl]
                h_ref[rows_t, re_sl] = nre
                h_ref[rows_t, im_sl] = nim
                out.append((nre, nim))
            return tuple(out)

        init = tuple((state_ref[:, n * STATE_TILE:n * STATE_TILE + LANES],
                      state_ref[:, n * STATE_TILE + LANES:(n + 1) * STATE_TILE]) for n in tiles)
        final = lax.fori_loop(0, tt, step, init, unroll=scan_unroll)
        for (hre, him), n in zip(final, tiles):
            state_ref[:, n * STATE_TILE:n * STATE_TILE + LANES] = hre
            state_ref[:, n * STATE_TILE + LANES:(n + 1) * STATE_TILE] = him

    chunk = tiles_per_chunk * STATE_TILE
    y = jnp.concatenate(
        [jnp.dot(h_ref[:, m * chunk:(m + 1) * chunk].astype(BF16), cmat_ref[m],
                 preferred_element_type=F32) for m in range(SSM_WIDTH // LANES)], axis=1)
    y = y + dskip_ref[...] * u
    y = jax.nn.gelu(y)
    gate = jnp.dot(y.astype(BF16), wglu_ref[0], preferred_element_type=F32) + bglu_ref[...]
    y = y * jax.nn.sigmoid(gate)
    z = uz_ref[:, :, SSM_WIDTH:]
    ys = jnp.swapaxes(y.reshape(tt, bsz, SSM_WIDTH), 0, 1) * (z * jax.nn.sigmoid(z))
    ys_ref[...] = ys.astype(ys_ref.dtype)


def _s5_branch(uz, bmat, cmat, lre, lim, d_skip, w_glu_all, b_glu, layer, *, tt=128, scan_unroll=128):
    bsz, seq, _ = uz.shape
    tiles_per_chunk = LANES // (2 * SSM_GROUP)
    kern = functools.partial(_s5_kernel, tt=tt, bsz=bsz, tiles_per_chunk=tiles_per_chunk,
                             scan_unroll=scan_unroll)
    full = lambda shape: pl.BlockSpec(shape, lambda i: (0,) * len(shape))
    return pl.pallas_call(
        kern,
        grid=(seq // tt,),
        in_specs=[
            pl.BlockSpec((bsz, tt, 2 * SSM_WIDTH), lambda i: (0, i, 0)),
            full(bmat.shape), full(cmat.shape), full(lre.shape), full(lim.shape),
            full(d_skip.shape),
            pl.BlockSpec((1,) + w_glu_all.shape[1:], lambda i: (layer, 0, 0)),
            full(b_glu.shape),
        ],
        out_specs=pl.BlockSpec((bsz, tt, SSM_WIDTH), lambda i: (0, i, 0)),
        out_shape=jax.ShapeDtypeStruct((bsz, seq, SSM_WIDTH), BF16),
        scratch_shapes=[
            pltpu.VMEM((tt * bsz, 2 * N_STATE), F32),
            pltpu.VMEM((tt * bsz, 2 * N_STATE), F32),
            pltpu.VMEM((bsz, 2 * N_STATE), F32),
        ],
        compiler_params=pltpu.CompilerParams(
            dimension_semantics=("arbitrary",), vmem_limit_bytes=VMEM_LIMIT),
        name="s5_branch",
    )(uz, bmat, cmat, lre, lim, d_skip, w_glu_all, b_glu)


def _attn_blocks(q, k, v, first):
    s = jnp.einsum('gqd,gkd->gqk', q, k, preferred_element_type=F32) * (HEAD_DIM ** -0.5 * LOG2E)
    qi = lax.broadcasted_iota(jnp.int32, s.shape[1:], 0)
    ki = lax.broadcasted_iota(jnp.int32, s.shape[1:], 1)
    valid = (ki <= qi) if first else ((ki >= qi) & (ki <= qi + SPAN))
    s = jnp.where(valid[None], s, -jnp.inf)
    m = jnp.max(s, axis=-1, keepdims=True)
    p = jnp.exp2(s - m)
    l = jnp.sum(p, axis=-1, keepdims=True)
    acc = jnp.einsum('gqk,gkd->gqd', p.astype(BF16), v, preferred_element_type=F32)
    return acc / l, m + jnp.log2(l)


def _attn_kernel(q1_ref, k1_ref, v1_ref, q2_ref, k2_ref, v2_ref, q3_ref, k3_ref, v3_ref,
                 za_ref, y_ref, out_ref, lse_ref):
    groups = ((q1_ref, k1_ref, v1_ref), (q2_ref, k2_ref, v2_ref), (q3_ref, k3_ref, v3_ref))
    for gi, (q_ref, k_ref, v_ref) in enumerate(groups):
        r = q_ref.shape[1]
        n_blk = q_ref.shape[3] // SPAN

        def put(blocks, o, lse, gi=gi, r=r):
            for g, (c, n) in enumerate(blocks):
                if r == 1:
                    rows = pl.ds(n * SPAN, SPAN)
                else:
                    rows = pl.ds(c + r * SPAN * n, SPAN, stride=r)
                out_ref[gi, rows, :] = o[g]
                lse_ref[gi, rows, :] = jnp.broadcast_to(lse[g], o[g].shape)

        first = [(c, 0) for c in range(r)]
        rest = [(c, n) for c in range(r) for n in range(1, n_blk)]
        for blocks, is_first in ((first, True), (rest, False)):
            for lo in range(0, len(blocks), ATTN_BATCH):
                chunk = blocks[lo:lo + ATTN_BATCH]
                q = jnp.stack([q_ref[0, c, 0, n * SPAN:(n + 1) * SPAN, :] for c, n in chunk])
                k0 = 0 if is_first else 1
                k = jnp.stack([k_ref[0, c, 0, (n - k0) * SPAN:(n + 1) * SPAN, :] for c, n in chunk])
                v = jnp.stack([v_ref[0, c, 0, (n - k0) * SPAN:(n + 1) * SPAN, :] for c, n in chunk])
                o, lse = _attn_blocks(q, k, v, is_first)
                put(chunk, o, lse)

    lse = lse_ref[...]
    mx = jnp.max(lse, axis=0)
    w = jnp.exp2(lse - mx[None])
    y = jnp.sum(w * out_ref[...], axis=0) / jnp.sum(w, axis=0)
    za = za_ref[0, 0].astype(F32)
    y_ref[0, 0] = (y * (za * jax.nn.sigmoid(za))).astype(y_ref.dtype)


def _attention(o1, o2, o3, za):
    bsz, _, _, seq, _ = o1.shape

    def specs(arr):
        _, r, _, n, _ = arr.shape
        return [pl.BlockSpec((1, r, 1, n, HEAD_DIM),
                             lambda b, h, part=part: (b, 0, part * HEADS_PER_GROUP + h, 0, 0))
                for part in range(3)]

    per_head = pl.BlockSpec((1, 1, seq, HEAD_DIM), lambda b, h: (b, h, 0, 0))

    return pl.pallas_call(
        _attn_kernel,
        grid=(bsz, HEADS_PER_GROUP),
        in_specs=specs(o1) + specs(o2) + specs(o3) + [per_head],
        out_specs=per_head,
        out_shape=jax.ShapeDtypeStruct((bsz, HEADS_PER_GROUP, seq, HEAD_DIM), BF16),
        scratch_shapes=[
            pltpu.VMEM((3, seq, HEAD_DIM), F32),
            pltpu.VMEM((3, seq, HEAD_DIM), F32),
        ],
        compiler_params=pltpu.CompilerParams(
            dimension_semantics=("arbitrary", "arbitrary"), vmem_limit_bytes=VMEM_LIMIT),
        name="dilated_attention",
    )(o1, o1, o1, o2, o2, o2, o3, o3, o3, za)


def _merge_kernel(x_ref, ys_ref, ya_ref, gs_ref, ga_ref, wbs_ref, wba_ref, wout_ref, pg_ref, o_ref):
    ms = jnp.dot(ys_ref[0], wbs_ref[0], preferred_element_type=F32)
    ya = jnp.concatenate([ya_ref[0, hh] for hh in range(HEADS_PER_GROUP)], axis=1)
    ma = jnp.dot(ya, wba_ref[0], preferred_element_type=F32)
    merged = (jax.nn.sigmoid(gs_ref[0].astype(F32)) * ms
              + jax.nn.sigmoid(ga_ref[0].astype(F32)) * ma)
    out = jnp.dot(merged.astype(BF16), wout_ref[0], preferred_element_type=F32)
    o_ref[0] = x_ref[0] + _rmsnorm_f32(out, pg_ref[0])


def _merge(x, ys, ya, gates, wbs_all, wba_all, wout_all, pg_all, layer, *, tm=512):
    bsz, seq, d_model = x.shape
    row = lambda width, blk: pl.BlockSpec((1, tm, width), lambda b, i: (b, i, blk))
    per_layer = lambda arr: pl.BlockSpec((1,) + arr.shape[1:], lambda b, i: (layer, 0, 0))
    return pl.pallas_call(
        _merge_kernel,
        grid=(bsz, seq // tm),
        in_specs=[
            row(d_model, 0), row(SSM_WIDTH, 0),
            pl.BlockSpec((1, HEADS_PER_GROUP, tm, HEAD_DIM), lambda b, i: (b, 0, i, 0)),
            row(d_model, 0),
            row(d_model, 1),
            per_layer(wbs_all), per_layer(wba_all), per_layer(wout_all), per_layer(pg_all),
        ],
        out_specs=row(d_model, 0),
        out_shape=jax.ShapeDtypeStruct((bsz, seq, d_model), F32),
        compiler_params=pltpu.CompilerParams(
            dimension_semantics=("arbitrary", "arbitrary"), vmem_limit_bytes=VMEM_LIMIT),
        name="merge_out",
    )(x, ys, ya, gates, gates, wbs_all, wba_all, wout_all, pg_all)


def _ssm_params(lam_re, lam_im, log_dt, b_re, b_im, c_re, c_im):
    depth = lam_re.shape[0]
    a = jnp.minimum(lam_re, -1e-4)
    b = lam_im
    dt = jnp.exp(log_dt)[..., None]
    mag = jnp.exp(a * dt)
    lbr = mag * jnp.cos(b * dt)
    lbi = mag * jnp.sin(b * dt)
    nr, ni = lbr - 1.0, lbi
    den = a * a + b * b
    kr = (nr * a + ni * b) / den
    ki = (ni * a - nr * b) / den
    bbr = kr[..., None] * b_re - ki[..., None] * b_im
    bbi = kr[..., None] * b_im + ki[..., None] * b_re
    per_block = LANES // (2 * SSM_GROUP)
    eye2 = jnp.eye(2, dtype=F32)
    slot = jax.nn.one_hot(jnp.arange(N_STATE_TILES) % per_block, per_block, dtype=F32)
    b5 = jnp.stack([bbr, bbi], axis=1).reshape(depth, 2, N_STATE_TILES, 2, SSM_STATE, SSM_GROUP)
    bblk = jnp.einsum('langpc,gh->lngcahp', b5, eye2).reshape(
        depth, N_STATE_TILES, 2 * SSM_GROUP, STATE_TILE)
    bmat = (bblk[:, :, None] * slot[None, :, :, None, None]).reshape(
        depth, N_STATE_TILES, LANES, STATE_TILE).astype(BF16)
    c5 = jnp.stack([c_re, -c_im], axis=1).reshape(depth, 2, N_STATE_TILES, 2, SSM_GROUP, SSM_STATE)
    cblk = jnp.einsum('langcp,gh->lnagphc', c5, eye2).reshape(
        depth, N_STATE_TILES, STATE_TILE, 2 * SSM_GROUP)
    cmat = (cblk[:, :, :, None, :] * slot[None, :, None, :, None]).reshape(
        depth, N_STATE_TILES // per_block, per_block * STATE_TILE, LANES).astype(BF16)
    return bmat, cmat, lbr.reshape(depth, 1, N_STATE), lbi.reshape(depth, 1, N_STATE)


def kernel(x, pre_norm_g, w_in, lambda_re, lambda_im, log_dt, b_re, b_im, c_re, c_im, d_skip, w_glu, b_glu, w_branch_s, w_branch_a, w_out, post_norm_g):
    depth = w_in.shape[0]
    assert x.shape[-1] == D_MODEL and w_in.shape[-1] == COL_GA + D_MODEL
    w_in_bf = w_in.astype(BF16)
    w_glu_bf = w_glu.astype(BF16)
    wbs_bf, wba_bf, wout_bf = (w.astype(BF16) for w in (w_branch_s, w_branch_a, w_out))
    pre_g = pre_norm_g[:, None, :]
    post_g = post_norm_g[:, None, :]
    bmat, cmat, lre, lim = _ssm_params(lambda_re, lambda_im, log_dt, b_re, b_im, c_re, c_im)
    for l in range(depth):
        o1, o2, o3, uz, gates, za = _in_projection(x, pre_g, w_in_bf, l)
        ys = _s5_branch(uz, bmat[l], cmat[l], lre[l], lim[l], d_skip[l][None, :], w_glu_bf,
                        b_glu[l][None, :], l)
        ya = _attention(o1, o2, o3, za)
        x = _merge(x, ys, ya, gates, wbs_bf, wba_bf, wout_bf, post_g, l)
    return x
```

```python
import functools

import jax
import jax.numpy as jnp
from jax import lax
from jax.experimental import pallas as pl
from jax.experimental.pallas import tpu as pltpu

F32 = jnp.float32
BF16 = jnp.bfloat16

RMS_EPS = 1e-6
LOG2E = 1.4426950408889634
SSM_WIDTH = 512
SSM_GROUP = 16
SSM_GROUPS = 32
SSM_STATE = 64
N_STATE = SSM_GROUPS * SSM_STATE
HEAD_DIM = 128
HEADS_PER_GROUP = 4
SPAN = 128
ATTN_WIDTH = HEADS_PER_GROUP * HEAD_DIM
LANES = 128
VMEM_LIMIT = 56 * 1024 * 1024

PROJ_TN = 512
D_MODEL = 1024
COL_U, COL_ZS = 0, SSM_WIDTH
COL_Q = 2 * SSM_WIDTH
COL_K = COL_Q + 3 * ATTN_WIDTH
COL_V = COL_K + 3 * ATTN_WIDTH
COL_ZA = COL_V + 3 * ATTN_WIDTH
COL_GS = COL_ZA + ATTN_WIDTH
COL_GA = COL_GS + D_MODEL
UZ_SRC = (COL_U, COL_ZS)
GATE_SRC = (COL_GS, COL_GS + PROJ_TN, COL_GA, COL_GA + PROJ_TN)
GATE_WIDTH = len(GATE_SRC) * PROJ_TN
QKV_SLOTS = 3 * HEADS_PER_GROUP
ATTN_BATCH = 16
STATE_TILE = 2 * LANES
N_STATE_TILES = 2 * N_STATE // STATE_TILE


def _rmsnorm_f32(x, g):
    inv = lax.rsqrt(jnp.mean(x * x, axis=-1, keepdims=True) + RMS_EPS)
    return x * inv * g


def _proj_kernel(x_ref, g_ref, w_ref, o1_ref, o2_ref, o3_ref, uz_ref, gate_ref, za_ref,
                 hf_ref, hn_ref, h4_ref, h16_ref, *, tm):
    d_model = x_ref.shape[-1]
    n_slab = d_model // LANES
    hf = _rmsnorm_f32(x_ref[0], g_ref[0])
    hn_ref[...] = hf.astype(BF16)
    for k in range(n_slab):
        hf_ref[k] = hf[:, k * LANES:(k + 1) * LANES]
    for r, dst in ((4, h4_ref), (16, h16_ref)):
        rows = tm // r
        for c in range(r):
            for k in range(n_slab):
                dst[c * rows:(c + 1) * rows, k * LANES:(k + 1) * LANES] = (
                    hf_ref[k, pl.ds(c, rows, stride=r), :].astype(BF16))

    def tile(h_ref, col):
        return jnp.dot(h_ref[...], w_ref[0, :, col:col + PROJ_TN], preferred_element_type=F32)

    def heads(t):
        return [t[:, hh * HEAD_DIM:(hh + 1) * HEAD_DIM] for hh in range(HEADS_PER_GROUP)]

    for j, base in enumerate((COL_Q, COL_K, COL_V)):
        for gi, (o_ref, h_ref, r) in enumerate(((o1_ref, hn_ref, 1), (o2_ref, h4_ref, 4),
                                               (o3_ref, h16_ref, 16))):
            t = tile(h_ref, base + gi * ATTN_WIDTH).astype(BF16)
            for hh, th in enumerate(heads(t)):
                o_ref[0, :, j * HEADS_PER_GROUP + hh, :, :] = th.reshape(r, tm // r, HEAD_DIM)
    for hh, th in enumerate(heads(tile(hn_ref, COL_ZA).astype(BF16))):
        za_ref[0, hh] = th
    for jj, col in enumerate(UZ_SRC):
        uz_ref[0, :, jj * PROJ_TN:(jj + 1) * PROJ_TN] = tile(hn_ref, col)
    for jj, col in enumerate(GATE_SRC):
        gate_ref[0, :, jj * PROJ_TN:(jj + 1) * PROJ_TN] = tile(hn_ref, col).astype(gate_ref.dtype)


def _in_projection(x, g_all, w_all, layer, *, tm=512):
    bsz, seq, d_model = x.shape
    n_i = seq // tm
    kern = functools.partial(_proj_kernel, tm=tm)
    qkv = lambda r: (1, r, QKV_SLOTS, tm // r, HEAD_DIM)
    qkv_all = lambda r: (bsz, r, QKV_SLOTS, seq // r, HEAD_DIM)
    return pl.pallas_call(
        kern,
        grid=(bsz, n_i),
        in_specs=[
            pl.BlockSpec((1, tm, d_model), lambda b, i: (b, i, 0)),
            pl.BlockSpec((1, 1, d_model), lambda b, i: (layer, 0, 0)),
            pl.BlockSpec((1,) + w_all.shape[1:], lambda b, i: (layer, 0, 0),
                         pipeline_mode=pl.Buffered(1)),
        ],
        out_specs=[
            pl.BlockSpec(qkv(1), lambda b, i: (b, 0, 0, i, 0)),
            pl.BlockSpec(qkv(4), lambda b, i: (b, 0, 0, i, 0)),
            pl.BlockSpec(qkv(16), lambda b, i: (b, 0, 0, i, 0)),
            pl.BlockSpec((1, tm, 2 * SSM_WIDTH), lambda b, i: (b, i, 0)),
            pl.BlockSpec((1, tm, GATE_WIDTH), lambda b, i: (b, i, 0)),
            pl.BlockSpec((1, HEADS_PER_GROUP, tm, HEAD_DIM), lambda b, i: (b, 0, i, 0)),
        ],
        out_shape=[
            jax.ShapeDtypeStruct(qkv_all(1), BF16),
            jax.ShapeDtypeStruct(qkv_all(4), BF16),
            jax.ShapeDtypeStruct(qkv_all(16), BF16),
            jax.ShapeDtypeStruct((bsz, seq, 2 * SSM_WIDTH), F32),
            jax.ShapeDtypeStruct((bsz, seq, GATE_WIDTH), BF16),
            jax.ShapeDtypeStruct((bsz, HEADS_PER_GROUP, seq, HEAD_DIM), BF16),
        ],
        scratch_shapes=[
            pltpu.VMEM((d_model // LANES, tm, LANES), F32),
            pltpu.VMEM((tm, d_model), BF16),
            pltpu.VMEM((tm, d_model), BF16),
            pltpu.VMEM((tm, d_model), BF16),
        ],
        compiler_params=pltpu.CompilerParams(
            dimension_semantics=("arbitrary", "arbitrary"),
            vmem_limit_bytes=VMEM_LIMIT),
        name="in_projection",
    )(x, g_all, w_all)


def _s5_kernel(uz_ref, bmat_ref, cmat_ref, lre_ref, lim_ref, dskip_ref, wglu_ref, bglu_ref,
               ys_ref, drive_ref, h_ref, state_ref, *, tt, bsz, tiles_per_chunk, scan_unroll):
    @pl.when(pl.program_id(0) == 0)
    def _():
        state_ref[...] = jnp.zeros_like(state_ref)

    rows = tt * bsz
    u = jnp.swapaxes(uz_ref[:, :, :SSM_WIDTH], 0, 1).reshape(rows, SSM_WIDTH)
    u_bf = u.astype(BF16)
    for n in range(N_STATE_TILES):
        blk = n // tiles_per_chunk
        drive_ref[:, n * STATE_TILE:(n + 1) * STATE_TILE] = jnp.dot(
            u_bf[:, blk * LANES:(blk + 1) * LANES], bmat_ref[n], preferred_element_type=F32)

    for n0 in range(0, N_STATE_TILES, tiles_per_chunk):
        tiles = range(n0, n0 + tiles_per_chunk)
        lam = [(jnp.broadcast_to(lre_ref[:, n * LANES:(n + 1) * LANES], (bsz, LANES)),
                jnp.broadcast_to(lim_ref[:, n * LANES:(n + 1) * LANES], (bsz, LANES))) for n in tiles]

        def step(t, carry, tiles=tiles, lam=lam):
            rows_t = pl.ds(pl.multiple_of(t * bsz, bsz), bsz)
            out = []
            for (hre, him), (lre, lim), n in zip(carry, lam, tiles):
                re_sl = slice(n * STATE_TILE, n * STATE_TILE + LANES)
                im_sl = slice(n * STATE_TILE + LANES, (n + 1) * STATE_TILE)
                nre = lre * hre - lim * him + drive_ref[rows_t, re_sl]
                nim = lre * him + lim * hre + drive_ref[rows_t, im_sl]
                h_ref[rows_t, re_sl] = nre
                h_ref[rows_t, im_sl] = nim
                out.append((nre, nim))
            return tuple(out)

        init = tuple((state_ref[:, n * STATE_TILE:n * STATE_TILE + LANES],
                      state_ref[:, n * STATE_TILE + LANES:(n + 1) * STATE_TILE]) for n in tiles)
        final = lax.fori_loop(0, tt, step, init, unroll=scan_unroll)
        for (hre, him), n in zip(final, tiles):
            state_ref[:, n * STATE_TILE:n * STATE_TILE + LANES] = hre
            state_ref[:, n * STATE_TILE + LANES:(n + 1) * STATE_TILE] = him

    chunk = tiles_per_chunk * STATE_TILE
    y = jnp.concatenate(
        [jnp.dot(h_ref[:, m * chunk:(m + 1) * chunk].astype(BF16), cmat_ref[m],
                 preferred_element_type=F32) for m in range(SSM_WIDTH // LANES)], axis=1)
    y = y + dskip_ref[...] * u
    y = jax.nn.gelu(y)
    gate = jnp.dot(y.astype(BF16), wglu_ref[0], preferred_element_type=F32) + bglu_ref[...]
    y = y * jax.nn.sigmoid(gate)
    z = uz_ref[:, :, SSM_WIDTH:]
    ys = jnp.swapaxes(y.reshape(tt, bsz, SSM_WIDTH), 0, 1) * (z * jax.nn.sigmoid(z))
    ys_ref[...] = ys.astype(ys_ref.dtype)


def _s5_branch(uz, bmat, cmat, lre, lim, d_skip, w_glu_all, b_glu, layer, *, tt=128, scan_unroll=128):
    bsz, seq, _ = uz.shape
    tiles_per_chunk = LANES // (2 * SSM_GROUP)
    kern = functools.partial(_s5_kernel, tt=tt, bsz=bsz, tiles_per_chunk=tiles_per_chunk,
                             scan_unroll=scan_unroll)
    full = lambda shape: pl.BlockSpec(shape, lambda i: (0,) * len(shape))
    return pl.pallas_call(
        kern,
        grid=(seq // tt,),
        in_specs=[
            pl.BlockSpec((bsz, tt, 2 * SSM_WIDTH), lambda i: (0, i, 0)),
            full(bmat.shape), full(cmat.shape), full(lre.shape), full(lim.shape),
            full(d_skip.shape),
            pl.BlockSpec((1,) + w_glu_all.shape[1:], lambda i: (layer, 0, 0)),
            full(b_glu.shape),
        ],
        out_specs=pl.BlockSpec((bsz, tt, SSM_WIDTH), lambda i: (0, i, 0)),
        out_shape=jax.ShapeDtypeStruct((bsz, seq, SSM_WIDTH), BF16),
        scratch_shapes=[
            pltpu.VMEM((tt * bsz, 2 * N_STATE), F32),
            pltpu.VMEM((tt * bsz, 2 * N_STATE), F32),
            pltpu.VMEM((bsz, 2 * N_STATE), F32),
        ],
        compiler_params=pltpu.CompilerParams(
            dimension_semantics=("arbitrary",), vmem_limit_bytes=VMEM_LIMIT),
        name="s5_branch",
    )(uz, bmat, cmat, lre, lim, d_skip, w_glu_all, b_glu)


def _attn_blocks(q, k, v, first):
    s = jnp.einsum('gqd,gkd->gqk', q, k, preferred_element_type=F32) * (HEAD_DIM ** -0.5 * LOG2E)
    qi = lax.broadcasted_iota(jnp.int32, s.shape[1:], 0)
    ki = lax.broadcasted_iota(jnp.int32, s.shape[1:], 1)
    valid = (ki <= qi) if first else ((ki >= qi) & (ki <= qi + SPAN))
    s = jnp.where(valid[None], s, -jnp.inf)
    m = jnp.max(s, axis=-1, keepdims=True)
    p = jnp.exp2(s - m)
    l = jnp.sum(p, axis=-1, keepdims=True)
    acc = jnp.einsum('gqk,gkd->gqd', p.astype(BF16), v, preferred_element_type=F32)
    return acc / l, m + jnp.log2(l)


def _attn_kernel(q1_ref, k1_ref, v1_ref, q2_ref, k2_ref, v2_ref, q3_ref, k3_ref, v3_ref,
                 za_ref, y_ref, out_ref, lse_ref):
    groups = ((q1_ref, k1_ref, v1_ref), (q2_ref, k2_ref, v2_ref), (q3_ref, k3_ref, v3_ref))
    for gi, (q_ref, k_ref, v_ref) in enumerate(groups):
        r = q_ref.shape[1]
        n_blk = q_ref.shape[3] // SPAN

        def put(blocks, o, lse, gi=gi, r=r):
            for g, (c, n) in enumerate(blocks):
                if r == 1:
                    rows = pl.ds(n * SPAN, SPAN)
                else:
                    rows = pl.ds(c + r * SPAN * n, SPAN, stride=r)
                out_ref[gi, rows, :] = o[g]
                lse_ref[gi, rows, :] = jnp.broadcast_to(lse[g], o[g].shape)

        first = [(c, 0) for c in range(r)]
        rest = [(c, n) for c in range(r) for n in range(1, n_blk)]
        for blocks, is_first in ((first, True), (rest, False)):
            for lo in range(0, len(blocks), ATTN_BATCH):
                chunk = blocks[lo:lo + ATTN_BATCH]
                q = jnp.stack([q_ref[0, c, 0, n * SPAN:(n + 1) * SPAN, :] for c, n in chunk])
                k0 = 0 if is_first else 1
                k = jnp.stack([k_ref[0, c, 0, (n - k0) * SPAN:(n + 1) * SPAN, :] for c, n in chunk])
                v = jnp.stack([v_ref[0, c, 0, (n - k0) * SPAN:(n + 1) * SPAN, :] for c, n in chunk])
                o, lse = _attn_blocks(q, k, v, is_first)
                put(chunk, o, lse)

    lse = lse_ref[...]
    mx = jnp.max(lse, axis=0)
    w = jnp.exp2(lse - mx[None])
    y = jnp.sum(w * out_ref[...], axis=0) / jnp.sum(w, axis=0)
    za = za_ref[0, 0].astype(F32)
    y_ref[0, 0] = (y * (za * jax.nn.sigmoid(za))).astype(y_ref.dtype)


def _attention(o1, o2, o3, za):
    bsz, _, _, seq, _ = o1.shape

    def specs(arr):
        _, r, _, n, _ = arr.shape
        return [pl.BlockSpec((1, r, 1, n, HEAD_DIM),
                             lambda b, h, part=part: (b, 0, part * HEADS_PER_GROUP + h, 0, 0))
                for part in range(3)]

    per_head = pl.BlockSpec((1, 1, seq, HEAD_DIM), lambda b, h: (b, h, 0, 0))

    return pl.pallas_call(
        _attn_kernel,
        grid=(bsz, HEADS_PER_GROUP),
        in_specs=specs(o1) + specs(o2) + specs(o3) + [per_head],
        out_specs=per_head,
        out_shape=jax.ShapeDtypeStruct((bsz, HEADS_PER_GROUP, seq, HEAD_DIM), BF16),
        scratch_shapes=[
            pltpu.VMEM((3, seq, HEAD_DIM), F32),
            pltpu.VMEM((3, seq, HEAD_DIM), F32),
        ],
        compiler_params=pltpu.CompilerParams(
            dimension_semantics=("arbitrary", "arbitrary"), vmem_limit_bytes=VMEM_LIMIT),
        name="dilated_attention",
    )(o1, o1, o1, o2, o2, o2, o3, o3, o3, za)


def _merge_kernel(x_ref, ys_ref, ya_ref, gs_ref, ga_ref, wbs_ref, wba_ref, wout_ref, pg_ref, o_ref):
    ms = jnp.dot(ys_ref[0], wbs_ref[0], preferred_element_type=F32)
    ya = jnp.concatenate([ya_ref[0, hh] for hh in range(HEADS_PER_GROUP)], axis=1)
    ma = jnp.dot(ya, wba_ref[0], preferred_element_type=F32)
    merged = (jax.nn.sigmoid(gs_ref[0].astype(F32)) * ms
              + jax.nn.sigmoid(ga_ref[0].astype(F32)) * ma)
    out = jnp.dot(merged.astype(BF16), wout_ref[0], preferred_element_type=F32)
    o_ref[0] = x_ref[0] + _rmsnorm_f32(out, pg_ref[0])


def _merge(x, ys, ya, gates, wbs_all, wba_all, wout_all, pg_all, layer, *, tm=1024):
    bsz, seq, d_model = x.shape
    row = lambda width, blk: pl.BlockSpec((1, tm, width), lambda b, i: (b, i, blk))
    per_layer = lambda arr: pl.BlockSpec((1,) + arr.shape[1:], lambda b, i: (layer, 0, 0))
    return pl.pallas_call(
        _merge_kernel,
        grid=(bsz, seq // tm),
        in_specs=[
            row(d_model, 0), row(SSM_WIDTH, 0),
            pl.BlockSpec((1, HEADS_PER_GROUP, tm, HEAD_DIM), lambda b, i: (b, 0, i, 0)),
            row(d_model, 0),
            row(d_model, 1),
            per_layer(wbs_all), per_layer(wba_all), per_layer(wout_all), per_layer(pg_all),
        ],
        out_specs=row(d_model, 0),
        out_shape=jax.ShapeDtypeStruct((bsz, seq, d_model), F32),
        compiler_params=pltpu.CompilerParams(
            dimension_semantics=("arbitrary", "arbitrary"), vmem_limit_bytes=VMEM_LIMIT),
        name="merge_out",
    )(x, ys, ya, gates, gates, wbs_all, wba_all, wout_all, pg_all)


def _ssm_params(lam_re, lam_im, log_dt, b_re, b_im, c_re, c_im):
    depth = lam_re.shape[0]
    a = jnp.minimum(lam_re, -1e-4)
    b = lam_im
    dt = jnp.exp(log_dt)[..., None]
    mag = jnp.exp(a * dt)
    lbr = mag * jnp.cos(b * dt)
    lbi = mag * jnp.sin(b * dt)
    nr, ni = lbr - 1.0, lbi
    den = a * a + b * b
    kr = (nr * a + ni * b) / den
    ki = (ni * a - nr * b) / den
    bbr = kr[..., None] * b_re - ki[..., None] * b_im
    bbi = kr[..., None] * b_im + ki[..., None] * b_re
    per_block = LANES // (2 * SSM_GROUP)
    eye2 = jnp.eye(2, dtype=F32)
    slot = jax.nn.one_hot(jnp.arange(N_STATE_TILES) % per_block, per_block, dtype=F32)
    b5 = jnp.stack([bbr, bbi], axis=1).reshape(depth, 2, N_STATE_TILES, 2, SSM_STATE, SSM_GROUP)
    bblk = jnp.einsum('langpc,gh->lngcahp', b5, eye2).reshape(
        depth, N_STATE_TILES, 2 * SSM_GROUP, STATE_TILE)
    bmat = (bblk[:, :, None] * slot[None, :, :, None, None]).reshape(
        depth, N_STATE_TILES, LANES, STATE_TILE).astype(BF16)
    c5 = jnp.stack([c_re, -c_im], axis=1).reshape(depth, 2, N_STATE_TILES, 2, SSM_GROUP, SSM_STATE)
    cblk = jnp.einsum('langcp,gh->lnagphc', c5, eye2).reshape(
        depth, N_STATE_TILES, STATE_TILE, 2 * SSM_GROUP)
    cmat = (cblk[:, :, :, None, :] * slot[None, :, None, :, None]).reshape(
        depth, N_STATE_TILES // per_block, per_block * STATE_TILE, LANES).astype(BF16)
    return bmat, cmat, lbr.reshape(depth, 1, N_STATE), lbi.reshape(depth, 1, N_STATE)


def kernel(x, pre_norm_g, w_in, lambda_re, lambda_im, log_dt, b_re, b_im, c_re, c_im, d_skip, w_glu, b_glu, w_branch_s, w_branch_a, w_out, post_norm_g):
    depth = w_in.shape[0]
    assert x.shape[-1] == D_MODEL and w_in.shape[-1] == COL_GA + D_MODEL
    w_in_bf = w_in.astype(BF16)
    w_glu_bf = w_glu.astype(BF16)
    wbs_bf, wba_bf, wout_bf = (w.astype(BF16) for w in (w_branch_s, w_branch_a, w_out))
    pre_g = pre_norm_g[:, None, :]
    post_g = post_norm_g[:, None, :]
    bmat, cmat, lre, lim = _ssm_params(lambda_re, lambda_im, log_dt, b_re, b_im, c_re, c_im)
    for l in range(depth):
        o1, o2, o3, uz, gates, za = _in_projection(x, pre_g, w_in_bf, l)
        ys = _s5_branch(uz, bmat[l], cmat[l], lre[l], lim[l], d_skip[l][None, :], w_glu_bf,
                        b_glu[l][None, :], l)
        ya = _attention(o1, o2, o3, za)
        x = _merge(x, ys, ya, gates, wbs_bf, wba_bf, wout_bf, post_g, l)
    return x
```

```python
import functools

import jax
import jax.numpy as jnp
from jax import lax
from jax.experimental import pallas as pl
from jax.experimental.pallas import tpu as pltpu

F32 = jnp.float32
BF16 = jnp.bfloat16

RMS_EPS = 1e-6
LOG2E = 1.4426950408889634
SSM_WIDTH = 512
SSM_GROUP = 16
SSM_GROUPS = 32
SSM_STATE = 64
N_STATE = SSM_GROUPS * SSM_STATE
HEAD_DIM = 128
HEADS_PER_GROUP = 4
SPAN = 128
ATTN_WIDTH = HEADS_PER_GROUP * HEAD_DIM
LANES = 128
VMEM_LIMIT = 56 * 1024 * 1024

PROJ_TN = 512
D_MODEL = 1024
COL_U, COL_ZS = 0, SSM_WIDTH
COL_Q = 2 * SSM_WIDTH
COL_K = COL_Q + 3 * ATTN_WIDTH
COL_V = COL_K + 3 * ATTN_WIDTH
COL_ZA = COL_V + 3 * ATTN_WIDTH
COL_GS = COL_ZA + ATTN_WIDTH
COL_GA = COL_GS + D_MODEL
UZ_SRC = (COL_U, COL_ZS)
GATE_SRC = (COL_GS, COL_GS + PROJ_TN, COL_GA, COL_GA + PROJ_TN)
GATE_WIDTH = len(GATE_SRC) * PROJ_TN
QKV_SLOTS = 3 * HEADS_PER_GROUP
ATTN_BATCH = 16
ATTN_HEADS_PER_STEP = 2
STATE_TILE = 2 * LANES
N_STATE_TILES = 2 * N_STATE // STATE_TILE


def _rmsnorm_f32(x, g):
    inv = lax.rsqrt(jnp.mean(x * x, axis=-1, keepdims=True) + RMS_EPS)
    return x * inv * g


def _proj_kernel(x_ref, g_ref, w_ref, o1_ref, o2_ref, o3_ref, uz_ref, gate_ref, za_ref,
                 hf_ref, hn_ref, h4_ref, h16_ref, *, tm):
    d_model = x_ref.shape[-1]
    n_slab = d_model // LANES
    hf = _rmsnorm_f32(x_ref[0], g_ref[0])
    hn_ref[...] = hf.astype(BF16)
    for k in range(n_slab):
        hf_ref[k] = hf[:, k * LANES:(k + 1) * LANES]
    for r, dst in ((4, h4_ref), (16, h16_ref)):
        rows = tm // r
        for c in range(r):
            for k in range(n_slab):
                dst[c * rows:(c + 1) * rows, k * LANES:(k + 1) * LANES] = (
                    hf_ref[k, pl.ds(c, rows, stride=r), :].astype(BF16))

    def tile(h_ref, col):
        return jnp.dot(h_ref[...], w_ref[0, :, col:col + PROJ_TN], preferred_element_type=F32)

    def heads(t):
        return [t[:, hh * HEAD_DIM:(hh + 1) * HEAD_DIM] for hh in range(HEADS_PER_GROUP)]

    for j, base in enumerate((COL_Q, COL_K, COL_V)):
        for gi, (o_ref, h_ref, r) in enumerate(((o1_ref, hn_ref, 1), (o2_ref, h4_ref, 4),
                                               (o3_ref, h16_ref, 16))):
            t = tile(h_ref, base + gi * ATTN_WIDTH).astype(BF16)
            for hh, th in enumerate(heads(t)):
                o_ref[0, :, j * HEADS_PER_GROUP + hh, :, :] = th.reshape(r, tm // r, HEAD_DIM)
    for hh, th in enumerate(heads(tile(hn_ref, COL_ZA).astype(BF16))):
        za_ref[0, hh] = th
    for jj, col in enumerate(UZ_SRC):
        uz_ref[0, :, jj * PROJ_TN:(jj + 1) * PROJ_TN] = tile(hn_ref, col)
    for jj, col in enumerate(GATE_SRC):
        gate_ref[0, :, jj * PROJ_TN:(jj + 1) * PROJ_TN] = tile(hn_ref, col).astype(gate_ref.dtype)


def _in_projection(x, g_all, w_all, layer, *, tm=512):
    bsz, seq, d_model = x.shape
    n_i = seq // tm
    kern = functools.partial(_proj_kernel, tm=tm)
    qkv = lambda r: (1, r, QKV_SLOTS, tm // r, HEAD_DIM)
    qkv_all = lambda r: (bsz, r, QKV_SLOTS, seq // r, HEAD_DIM)
    return pl.pallas_call(
        kern,
        grid=(bsz, n_i),
        in_specs=[
            pl.BlockSpec((1, tm, d_model), lambda b, i: (b, i, 0)),
            pl.BlockSpec((1, 1, d_model), lambda b, i: (layer, 0, 0)),
            pl.BlockSpec((1,) + w_all.shape[1:], lambda b, i: (layer, 0, 0),
                         pipeline_mode=pl.Buffered(1)),
        ],
        out_specs=[
            pl.BlockSpec(qkv(1), lambda b, i: (b, 0, 0, i, 0)),
            pl.BlockSpec(qkv(4), lambda b, i: (b, 0, 0, i, 0)),
            pl.BlockSpec(qkv(16), lambda b, i: (b, 0, 0, i, 0)),
            pl.BlockSpec((1, tm, 2 * SSM_WIDTH), lambda b, i: (b, i, 0)),
            pl.BlockSpec((1, tm, GATE_WIDTH), lambda b, i: (b, i, 0)),
            pl.BlockSpec((1, HEADS_PER_GROUP, tm, HEAD_DIM), lambda b, i: (b, 0, i, 0)),
        ],
        out_shape=[
            jax.ShapeDtypeStruct(qkv_all(1), BF16),
            jax.ShapeDtypeStruct(qkv_all(4), BF16),
            jax.ShapeDtypeStruct(qkv_all(16), BF16),
            jax.ShapeDtypeStruct((bsz, seq, 2 * SSM_WIDTH), F32),
            jax.ShapeDtypeStruct((bsz, seq, GATE_WIDTH), BF16),
            jax.ShapeDtypeStruct((bsz, HEADS_PER_GROUP, seq, HEAD_DIM), BF16),
        ],
        scratch_shapes=[
            pltpu.VMEM((d_model // LANES, tm, LANES), F32),
            pltpu.VMEM((tm, d_model), BF16),
            pltpu.VMEM((tm, d_model), BF16),
            pltpu.VMEM((tm, d_model), BF16),
        ],
        compiler_params=pltpu.CompilerParams(
            dimension_semantics=("arbitrary", "arbitrary"),
            vmem_limit_bytes=VMEM_LIMIT),
        name="in_projection",
    )(x, g_all, w_all)


def _s5_kernel(uz_ref, bmat_ref, cmat_ref, lre_ref, lim_ref, dskip_ref, wglu_ref, bglu_ref,
               ys_ref, drive_ref, h_ref, state_ref, *, tt, bsz, tiles_per_chunk, scan_unroll):
    @pl.when(pl.program_id(0) == 0)
    def _():
        state_ref[...] = jnp.zeros_like(state_ref)

    rows = tt * bsz
    u = jnp.swapaxes(uz_ref[:, :, :SSM_WIDTH], 0, 1).reshape(rows, SSM_WIDTH)
    u_bf = u.astype(BF16)
    for n in range(N_STATE_TILES):
        blk = n // tiles_per_chunk
        drive_ref[:, n * STATE_TILE:(n + 1) * STATE_TILE] = jnp.dot(
            u_bf[:, blk * LANES:(blk + 1) * LANES], bmat_ref[n], preferred_element_type=F32)

    for n0 in range(0, N_STATE_TILES, tiles_per_chunk):
        tiles = range(n0, n0 + tiles_per_chunk)
        lam = [(jnp.broadcast_to(lre_ref[:, n * LANES:(n + 1) * LANES], (bsz, LANES)),
                jnp.broadcast_to(lim_ref[:, n * LANES:(n + 1) * LANES], (bsz, LANES))) for n in tiles]

        def step(t, carry, tiles=tiles, lam=lam):
            rows_t = pl.ds(pl.multiple_of(t * bsz, bsz), bsz)
            out = []
            for (hre, him), (lre, lim), n in zip(carry, lam, tiles):
                re_sl = slice(n * STATE_TILE, n * STATE_TILE + LANES)
                im_sl = slice(n * STATE_TILE + LANES, (n + 1) * STATE_TILE)
                nre = lre * hre - lim * him + drive_ref[rows_t, re_sl]
                nim = lre * him + lim * hre + drive_ref[rows_t, im_sl]
                h_ref[rows_t, re_sl] = nre
                h_ref[rows_t, im_sl] = nim
                out.append((nre, nim))
            return tuple(out)

        init = tuple((state_ref[:, n * STATE_TILE:n * STATE_TILE + LANES],
                      state_ref[:, n * STATE_TILE + LANES:(n + 1) * STATE_TILE]) for n in tiles)
        final = lax.fori_loop(0, tt, step, init, unroll=scan_unroll)
        for (hre, him), n in zip(final, tiles):
            state_ref[:, n * STATE_TILE:n * STATE_TILE + LANES] = hre
            state_ref[:, n * STATE_TILE + LANES:(n + 1) * STATE_TILE] = him

    chunk = tiles_per_chunk * STATE_TILE
    y = jnp.concatenate(
        [jnp.dot(h_ref[:, m * chunk:(m + 1) * chunk].astype(BF16), cmat_ref[m],
                 preferred_element_type=F32) for m in range(SSM_WIDTH // LANES)], axis=1)
    y = y + dskip_ref[...] * u
    y = jax.nn.gelu(y)
    gate = jnp.dot(y.astype(BF16), wglu_ref[0], preferred_element_type=F32) + bglu_ref[...]
    y = y * jax.nn.sigmoid(gate)
    z = uz_ref[:, :, SSM_WIDTH:]
    ys = jnp.swapaxes(y.reshape(tt, bsz, SSM_WIDTH), 0, 1) * (z * jax.nn.sigmoid(z))
    ys_ref[...] = ys.astype(ys_ref.dtype)


def _s5_branch(uz, bmat, cmat, lre, lim, d_skip, w_glu_all, b_glu, layer, *, tt=128, scan_unroll=128):
    bsz, seq, _ = uz.shape
    tiles_per_chunk = LANES // (2 * SSM_GROUP)
    kern = functools.partial(_s5_kernel, tt=tt, bsz=bsz, tiles_per_chunk=tiles_per_chunk,
                             scan_unroll=scan_unroll)
    full = lambda shape: pl.BlockSpec(shape, lambda i: (0,) * len(shape))
    return pl.pallas_call(
        kern,
        grid=(seq // tt,),
        in_specs=[
            pl.BlockSpec((bsz, tt, 2 * SSM_WIDTH), lambda i: (0, i, 0)),
            full(bmat.shape), full(cmat.shape), full(lre.shape), full(lim.shape),
            full(d_skip.shape),
            pl.BlockSpec((1,) + w_glu_all.shape[1:], lambda i: (layer, 0, 0)),
            full(b_glu.shape),
        ],
        out_specs=pl.BlockSpec((bsz, tt, SSM_WIDTH), lambda i: (0, i, 0)),
        out_shape=jax.ShapeDtypeStruct((bsz, seq, SSM_WIDTH), BF16),
        scratch_shapes=[
            pltpu.VMEM((tt * bsz, 2 * N_STATE), F32),
            pltpu.VMEM((tt * bsz, 2 * N_STATE), F32),
            pltpu.VMEM((bsz, 2 * N_STATE), F32),
        ],
        compiler_params=pltpu.CompilerParams(
            dimension_semantics=("arbitrary",), vmem_limit_bytes=VMEM_LIMIT),
        name="s5_branch",
    )(uz, bmat, cmat, lre, lim, d_skip, w_glu_all, b_glu)


def _attn_blocks(q, k, v, first):
    s = jnp.einsum('gqd,gkd->gqk', q, k, preferred_element_type=F32) * (HEAD_DIM ** -0.5 * LOG2E)
    qi = lax.broadcasted_iota(jnp.int32, s.shape[1:], 0)
    ki = lax.broadcasted_iota(jnp.int32, s.shape[1:], 1)
    valid = (ki <= qi) if first else ((ki >= qi) & (ki <= qi + SPAN))
    s = jnp.where(valid[None], s, -jnp.inf)
    m = jnp.max(s, axis=-1, keepdims=True)
    p = jnp.exp2(s - m)
    l = jnp.sum(p, axis=-1, keepdims=True)
    acc = jnp.einsum('gqk,gkd->gqd', p.astype(BF16), v, preferred_element_type=F32)
    return acc / l, m + jnp.log2(l)


def _attn_kernel(q1_ref, k1_ref, v1_ref, q2_ref, k2_ref, v2_ref, q3_ref, k3_ref, v3_ref,
                 za_ref, y_ref, out_ref, lse_ref):
    groups = ((q1_ref, k1_ref, v1_ref), (q2_ref, k2_ref, v2_ref), (q3_ref, k3_ref, v3_ref))
    for hh in range(za_ref.shape[1]):
        for gi, (q_ref, k_ref, v_ref) in enumerate(groups):
            r = q_ref.shape[1]
            n_blk = q_ref.shape[3] // SPAN

            def put(blocks, o, lse, gi=gi, r=r, hh=hh):
                for g, (c, n) in enumerate(blocks):
                    if r == 1:
                        rows = pl.ds(n * SPAN, SPAN)
                    else:
                        rows = pl.ds(c + r * SPAN * n, SPAN, stride=r)
                    out_ref[hh, gi, rows, :] = o[g]
                    lse_ref[hh, gi, rows, :] = jnp.broadcast_to(lse[g], o[g].shape)

            first = [(c, 0) for c in range(r)]
            rest = [(c, n) for c in range(r) for n in range(1, n_blk)]
            for blocks, is_first in ((first, True), (rest, False)):
                for lo in range(0, len(blocks), ATTN_BATCH):
                    chunk = blocks[lo:lo + ATTN_BATCH]
                    k0 = 0 if is_first else 1
                    q = jnp.stack([q_ref[0, c, hh, n * SPAN:(n + 1) * SPAN, :] for c, n in chunk])
                    k = jnp.stack([k_ref[0, c, hh, (n - k0) * SPAN:(n + 1) * SPAN, :] for c, n in chunk])
                    v = jnp.stack([v_ref[0, c, hh, (n - k0) * SPAN:(n + 1) * SPAN, :] for c, n in chunk])
                    o, lse = _attn_blocks(q, k, v, is_first)
                    put(chunk, o, lse)

        lse = lse_ref[hh]
        mx = jnp.max(lse, axis=0)
        w = jnp.exp2(lse - mx[None])
        y = jnp.sum(w * out_ref[hh], axis=0) / jnp.sum(w, axis=0)
        za = za_ref[0, hh].astype(F32)
        y_ref[0, hh] = (y * (za * jax.nn.sigmoid(za))).astype(y_ref.dtype)


def _attention(o1, o2, o3, za):
    bsz, _, _, seq, _ = o1.shape
    hps = ATTN_HEADS_PER_STEP
    steps = HEADS_PER_GROUP // hps

    def specs(arr):
        _, r, _, n, _ = arr.shape
        return [pl.BlockSpec((1, r, hps, n, HEAD_DIM),
                             lambda b, h, part=part: (b, 0, part * steps + h, 0, 0))
                for part in range(3)]

    per_head = pl.BlockSpec((1, hps, seq, HEAD_DIM), lambda b, h: (b, h, 0, 0))

    return pl.pallas_call(
        _attn_kernel,
        grid=(bsz, steps),
        in_specs=specs(o1) + specs(o2) + specs(o3) + [per_head],
        out_specs=per_head,
        out_shape=jax.ShapeDtypeStruct((bsz, HEADS_PER_GROUP, seq, HEAD_DIM), BF16),
        scratch_shapes=[
            pltpu.VMEM((hps, 3, seq, HEAD_DIM), F32),
            pltpu.VMEM((hps, 3, seq, HEAD_DIM), F32),
        ],
        compiler_params=pltpu.CompilerParams(
            dimension_semantics=("arbitrary", "arbitrary"), vmem_limit_bytes=VMEM_LIMIT),
        name="dilated_attention",
    )(o1, o1, o1, o2, o2, o2, o3, o3, o3, za)


def _merge_kernel(x_ref, ys_ref, ya_ref, gs_ref, ga_ref, wbs_ref, wba_ref, wout_ref, pg_ref, o_ref):
    ms = jnp.dot(ys_ref[0], wbs_ref[0], preferred_element_type=F32)
    ya = jnp.concatenate([ya_ref[0, hh] for hh in range(HEADS_PER_GROUP)], axis=1)
    ma = jnp.dot(ya, wba_ref[0], preferred_element_type=F32)
    merged = (jax.nn.sigmoid(gs_ref[0].astype(F32)) * ms
              + jax.nn.sigmoid(ga_ref[0].astype(F32)) * ma)
    out = jnp.dot(merged.astype(BF16), wout_ref[0], preferred_element_type=F32)
    o_ref[0] = x_ref[0] + _rmsnorm_f32(out, pg_ref[0])


def _merge(x, ys, ya, gates, wbs_all, wba_all, wout_all, pg_all, layer, *, tm=1024):
    bsz, seq, d_model = x.shape
    row = lambda width, blk: pl.BlockSpec((1, tm, width), lambda b, i: (b, i, blk))
    per_layer = lambda arr: pl.BlockSpec((1,) + arr.shape[1:], lambda b, i: (layer, 0, 0))
    return pl.pallas_call(
        _merge_kernel,
        grid=(bsz, seq // tm),
        in_specs=[
            row(d_model, 0), row(SSM_WIDTH, 0),
            pl.BlockSpec((1, HEADS_PER_GROUP, tm, HEAD_DIM), lambda b, i: (b, 0, i, 0)),
            row(d_model, 0),
            row(d_model, 1),
            per_layer(wbs_all), per_layer(wba_all), per_layer(wout_all), per_layer(pg_all),
        ],
        out_specs=row(d_model, 0),
        out_shape=jax.ShapeDtypeStruct((bsz, seq, d_model), F32),
        compiler_params=pltpu.CompilerParams(
            dimension_semantics=("arbitrary", "arbitrary"), vmem_limit_bytes=VMEM_LIMIT),
        name="merge_out",
    )(x, ys, ya, gates, gates, wbs_all, wba_all, wout_all, pg_all)


def _ssm_params(lam_re, lam_im, log_dt, b_re, b_im, c_re, c_im):
    depth = lam_re.shape[0]
    a = jnp.minimum(lam_re, -1e-4)
    b = lam_im
    dt = jnp.exp(log_dt)[..., None]
    mag = jnp.exp(a * dt)
    lbr = mag * jnp.cos(b * dt)
    lbi = mag * jnp.sin(b * dt)
    nr, ni = lbr - 1.0, lbi
    den = a * a + b * b
    kr = (nr * a + ni * b) / den
    ki = (ni * a - nr * b) / den
    bbr = kr[..., None] * b_re - ki[..., None] * b_im
    bbi = kr[..., None] * b_im + ki[..., None] * b_re
    per_block = LANES // (2 * SSM_GROUP)
    eye2 = jnp.eye(2, dtype=F32)
    slot = jax.nn.one_hot(jnp.arange(N_STATE_TILES) % per_block, per_block, dtype=F32)
    b5 = jnp.stack([bbr, bbi], axis=1).reshape(depth, 2, N_STATE_TILES, 2, SSM_STATE, SSM_GROUP)
    bblk = jnp.einsum('langpc,gh->lngcahp', b5, eye2).reshape(
        depth, N_STATE_TILES, 2 * SSM_GROUP, STATE_TILE)
    bmat = (bblk[:, :, None] * slot[None, :, :, None, None]).reshape(
        depth, N_STATE_TILES, LANES, STATE_TILE).astype(BF16)
    c5 = jnp.stack([c_re, -c_im], axis=1).reshape(depth, 2, N_STATE_TILES, 2, SSM_GROUP, SSM_STATE)
    cblk = jnp.einsum('langcp,gh->lnagphc', c5, eye2).reshape(
        depth, N_STATE_TILES, STATE_TILE, 2 * SSM_GROUP)
    cmat = (cblk[:, :, :, None, :] * slot[None, :, None, :, None]).reshape(
        depth, N_STATE_TILES // per_block, per_block * STATE_TILE, LANES).astype(BF16)
    return bmat, cmat, lbr.reshape(depth, 1, N_STATE), lbi.reshape(depth, 1, N_STATE)


def kernel(x, pre_norm_g, w_in, lambda_re, lambda_im, log_dt, b_re, b_im, c_re, c_im, d_skip, w_glu, b_glu, w_branch_s, w_branch_a, w_out, post_norm_g):
    depth = w_in.shape[0]
    assert x.shape[-1] == D_MODEL and w_in.shape[-1] == COL_GA + D_MODEL
    w_in_bf = w_in.astype(BF16)
    w_glu_bf = w_glu.astype(BF16)
    wbs_bf, wba_bf, wout_bf = (w.astype(BF16) for w in (w_branch_s, w_branch_a, w_out))
    pre_g = pre_norm_g[:, None, :]
    post_g = post_norm_g[:, None, :]
    bmat, cmat, lre, lim = _ssm_params(lambda_re, lambda_im, log_dt, b_re, b_im, c_re, c_im)
    for l in range(depth):
        o1, o2, o3, uz, gates, za = _in_projection(x, pre_g, w_in_bf, l)
        ys = _s5_branch(uz, bmat[l], cmat[l], lre[l], lim[l], d_skip[l][None, :], w_glu_bf,
                        b_glu[l][None, :], l)
        ya = _attention(o1, o2, o3, za)
        x = _merge(x, ys, ya, gates, wbs_bf, wba_bf, wout_bf, post_g, l)
    return x
```

```python
import functools

import jax
import jax.numpy as jnp
from jax import lax
from jax.experimental import pallas as pl
from jax.experimental.pallas import tpu as pltpu

F32 = jnp.float32
BF16 = jnp.bfloat16

RMS_EPS = 1e-6
LOG2E = 1.4426950408889634
SSM_WIDTH = 512
SSM_GROUP = 16
SSM_GROUPS = 32
SSM_STATE = 64
N_STATE = SSM_GROUPS * SSM_STATE
HEAD_DIM = 128
HEADS_PER_GROUP = 4
SPAN = 128
ATTN_WIDTH = HEADS_PER_GROUP * HEAD_DIM
LANES = 128
VMEM_LIMIT = 56 * 1024 * 1024

PROJ_TN = 512
D_MODEL = 1024
COL_U, COL_ZS = 0, SSM_WIDTH
COL_Q = 2 * SSM_WIDTH
COL_K = COL_Q + 3 * ATTN_WIDTH
COL_V = COL_K + 3 * ATTN_WIDTH
COL_ZA = COL_V + 3 * ATTN_WIDTH
COL_GS = COL_ZA + ATTN_WIDTH
COL_GA = COL_GS + D_MODEL
UZ_SRC = (COL_U, COL_ZS)
GATE_SRC = (COL_GS, COL_GS + PROJ_TN, COL_GA, COL_GA + PROJ_TN)
GATE_WIDTH = len(GATE_SRC) * PROJ_TN
QKV_SLOTS = 3 * HEADS_PER_GROUP
ATTN_BATCH = 16
STATE_TILE = 2 * LANES
N_STATE_TILES = 2 * N_STATE // STATE_TILE


def _rmsnorm_f32(x, g):
    inv = lax.rsqrt(jnp.mean(x * x, axis=-1, keepdims=True) + RMS_EPS)
    return x * inv * g


def _proj_kernel(x_ref, g_ref, w_ref, o1_ref, o2_ref, o3_ref, uz_ref, gate_ref, za_ref,
                 hf_ref, hn_ref, h4_ref, h16_ref, *, tm):
    d_model = x_ref.shape[-1]
    n_slab = d_model // LANES
    hf = _rmsnorm_f32(x_ref[0], g_ref[0])
    hn_ref[...] = hf.astype(BF16)
    for k in range(n_slab):
        hf_ref[k] = hf[:, k * LANES:(k + 1) * LANES]
    for r, dst in ((4, h4_ref), (16, h16_ref)):
        rows = tm // r
        for c in range(r):
            for k in range(n_slab):
                dst[c * rows:(c + 1) * rows, k * LANES:(k + 1) * LANES] = (
                    hf_ref[k, pl.ds(c, rows, stride=r), :].astype(BF16))

    def tile(h_ref, col):
        return jnp.dot(h_ref[...], w_ref[0, :, col:col + PROJ_TN], preferred_element_type=F32)

    def heads(t):
        return [t[:, hh * HEAD_DIM:(hh + 1) * HEAD_DIM] for hh in range(HEADS_PER_GROUP)]

    for j, base in enumerate((COL_Q, COL_K, COL_V)):
        for gi, (o_ref, h_ref, r) in enumerate(((o1_ref, hn_ref, 1), (o2_ref, h4_ref, 4),
                                               (o3_ref, h16_ref, 16))):
            t = tile(h_ref, base + gi * ATTN_WIDTH).astype(BF16)
            for hh, th in enumerate(heads(t)):
                o_ref[0, :, j * HEADS_PER_GROUP + hh, :, :] = th.reshape(r, tm // r, HEAD_DIM)
    for hh, th in enumerate(heads(tile(hn_ref, COL_ZA).astype(BF16))):
        za_ref[0, hh] = th
    for jj, col in enumerate(UZ_SRC):
        uz_ref[0, :, jj * PROJ_TN:(jj + 1) * PROJ_TN] = tile(hn_ref, col)
    for jj, col in enumerate(GATE_SRC):
        gate_ref[0, :, jj * PROJ_TN:(jj + 1) * PROJ_TN] = tile(hn_ref, col).astype(gate_ref.dtype)


def _in_projection(x, g_all, w_all, layer, *, tm=512):
    bsz, seq, d_model = x.shape
    n_i = seq // tm
    kern = functools.partial(_proj_kernel, tm=tm)
    qkv = lambda r: (1, r, QKV_SLOTS, tm // r, HEAD_DIM)
    qkv_all = lambda r: (bsz, r, QKV_SLOTS, seq // r, HEAD_DIM)
    return pl.pallas_call(
        kern,
        grid=(bsz, n_i),
        in_specs=[
            pl.BlockSpec((1, tm, d_model), lambda b, i: (b, i, 0)),
            pl.BlockSpec((1, 1, d_model), lambda b, i: (layer, 0, 0)),
            pl.BlockSpec((1,) + w_all.shape[1:], lambda b, i: (layer, 0, 0),
                         pipeline_mode=pl.Buffered(1)),
        ],
        out_specs=[
            pl.BlockSpec(qkv(1), lambda b, i: (b, 0, 0, i, 0)),
            pl.BlockSpec(qkv(4), lambda b, i: (b, 0, 0, i, 0)),
            pl.BlockSpec(qkv(16), lambda b, i: (b, 0, 0, i, 0)),
            pl.BlockSpec((1, tm, 2 * SSM_WIDTH), lambda b, i: (b, i, 0)),
            pl.BlockSpec((1, tm, GATE_WIDTH), lambda b, i: (b, i, 0)),
            pl.BlockSpec((1, HEADS_PER_GROUP, tm, HEAD_DIM), lambda b, i: (b, 0, i, 0)),
        ],
        out_shape=[
            jax.ShapeDtypeStruct(qkv_all(1), BF16),
            jax.ShapeDtypeStruct(qkv_all(4), BF16),
            jax.ShapeDtypeStruct(qkv_all(16), BF16),
            jax.ShapeDtypeStruct((bsz, seq, 2 * SSM_WIDTH), F32),
            jax.ShapeDtypeStruct((bsz, seq, GATE_WIDTH), BF16),
            jax.ShapeDtypeStruct((bsz, HEADS_PER_GROUP, seq, HEAD_DIM), BF16),
        ],
        scratch_shapes=[
            pltpu.VMEM((d_model // LANES, tm, LANES), F32),
            pltpu.VMEM((tm, d_model), BF16),
            pltpu.VMEM((tm, d_model), BF16),
            pltpu.VMEM((tm, d_model), BF16),
        ],
        compiler_params=pltpu.CompilerParams(
            dimension_semantics=("arbitrary", "arbitrary"),
            vmem_limit_bytes=VMEM_LIMIT),
        name="in_projection",
    )(x, g_all, w_all)


def _s5_kernel(uz_ref, bmat_ref, cmat_ref, lre_ref, lim_ref, dskip_ref, wglu_ref, bglu_ref,
               ys_ref, drive_ref, h_ref, state_ref, *, tt, bsz, tiles_per_chunk, scan_unroll):
    @pl.when(pl.program_id(0) == 0)
    def _():
        state_ref[...] = jnp.zeros_like(state_ref)

    rows = tt * bsz
    u = jnp.swapaxes(uz_ref[:, :, :SSM_WIDTH], 0, 1).reshape(rows, SSM_WIDTH)
    u_bf = u.astype(BF16)
    for n in range(N_STATE_TILES):
        blk = n // tiles_per_chunk
        drive_ref[:, n * STATE_TILE:(n + 1) * STATE_TILE] = jnp.dot(
            u_bf[:, blk * LANES:(blk + 1) * LANES], bmat_ref[n], preferred_element_type=F32)

    for n0 in range(0, N_STATE_TILES, tiles_per_chunk):
        tiles = range(n0, n0 + tiles_per_chunk)
        lam = [(jnp.broadcast_to(lre_ref[:, n * LANES:(n + 1) * LANES], (bsz, LANES)),
                jnp.broadcast_to(lim_ref[:, n * LANES:(n + 1) * LANES], (bsz, LANES))) for n in tiles]

        def step(t, carry, tiles=tiles, lam=lam):
            rows_t = pl.ds(pl.multiple_of(t * bsz, bsz), bsz)
            out = []
            for (hre, him), (lre, lim), n in zip(carry, lam, tiles):
                re_sl = slice(n * STATE_TILE, n * STATE_TILE + LANES)
                im_sl = slice(n * STATE_TILE + LANES, (n + 1) * STATE_TILE)
                nre = lre * hre - lim * him + drive_ref[rows_t, re_sl]
                nim = lre * him + lim * hre + drive_ref[rows_t, im_sl]
                h_ref[rows_t, re_sl] = nre
                h_ref[rows_t, im_sl] = nim
                out.append((nre, nim))
            return tuple(out)

        init = tuple((state_ref[:, n * STATE_TILE:n * STATE_TILE + LANES],
                      state_ref[:, n * STATE_TILE + LANES:(n + 1) * STATE_TILE]) for n in tiles)
        final = lax.fori_loop(0, tt, step, init, unroll=scan_unroll)
        for (hre, him), n in zip(final, tiles):
            state_ref[:, n * STATE_TILE:n * STATE_TILE + LANES] = hre
            state_ref[:, n * STATE_TILE + LANES:(n + 1) * STATE_TILE] = him

    chunk = tiles_per_chunk * STATE_TILE
    y = jnp.concatenate(
        [jnp.dot(h_ref[:, m * chunk:(m + 1) * chunk].astype(BF16), cmat_ref[m],
                 preferred_element_type=F32) for m in range(SSM_WIDTH // LANES)], axis=1)
    y = y + dskip_ref[...] * u
    y = jax.nn.gelu(y)
    gate = jnp.dot(y.astype(BF16), wglu_ref[0], preferred_element_type=F32) + bglu_ref[...]
    y = y * jax.nn.sigmoid(gate)
    z = uz_ref[:, :, SSM_WIDTH:]
    ys = jnp.swapaxes(y.reshape(tt, bsz, SSM_WIDTH), 0, 1) * (z * jax.nn.sigmoid(z))
    ys_ref[...] = ys.astype(ys_ref.dtype)


def _s5_branch(uz, bmat, cmat, lre, lim, d_skip, w_glu_all, b_glu, layer, *, tt=128, scan_unroll=128):
    bsz, seq, _ = uz.shape
    tiles_per_chunk = LANES // (2 * SSM_GROUP)
    kern = functools.partial(_s5_kernel, tt=tt, bsz=bsz, tiles_per_chunk=tiles_per_chunk,
                             scan_unroll=scan_unroll)
    full = lambda shape: pl.BlockSpec(shape, lambda i: (0,) * len(shape))
    return pl.pallas_call(
        kern,
        grid=(seq // tt,),
        in_specs=[
            pl.BlockSpec((bsz, tt, 2 * SSM_WIDTH), lambda i: (0, i, 0)),
            full(bmat.shape), full(cmat.shape), full(lre.shape), full(lim.shape),
            full(d_skip.shape),
            pl.BlockSpec((1,) + w_glu_all.shape[1:], lambda i: (layer, 0, 0)),
            full(b_glu.shape),
        ],
        out_specs=pl.BlockSpec((bsz, tt, SSM_WIDTH), lambda i: (0, i, 0)),
        out_shape=jax.ShapeDtypeStruct((bsz, seq, SSM_WIDTH), BF16),
        scratch_shapes=[
            pltpu.VMEM((tt * bsz, 2 * N_STATE), F32),
            pltpu.VMEM((tt * bsz, 2 * N_STATE), F32),
            pltpu.VMEM((bsz, 2 * N_STATE), F32),
        ],
        compiler_params=pltpu.CompilerParams(
            dimension_semantics=("arbitrary",), vmem_limit_bytes=VMEM_LIMIT),
        name="s5_branch",
    )(uz, bmat, cmat, lre, lim, d_skip, w_glu_all, b_glu)


def _attn_blocks(q, k, v, first):
    s = jnp.einsum('gqd,gkd->gqk', q, k, preferred_element_type=F32) * (HEAD_DIM ** -0.5 * LOG2E)
    qi = lax.broadcasted_iota(jnp.int32, s.shape[1:], 0)
    ki = lax.broadcasted_iota(jnp.int32, s.shape[1:], 1)
    valid = (ki <= qi) if first else ((ki >= qi) & (ki <= qi + SPAN))
    s = jnp.where(valid[None], s, -jnp.inf)
    m = jnp.max(s, axis=-1, keepdims=True)
    p = jnp.exp2(s - m)
    l = jnp.sum(p, axis=-1, keepdims=True)
    acc = jnp.einsum('gqk,gkd->gqd', p.astype(BF16), v, preferred_element_type=F32)
    return acc / l, m + jnp.log2(l)


def _row_pitch(r):
    return r if r % 16 else r + r // 2


def _attn_kernel(q1_ref, k1_ref, v1_ref, q2_ref, k2_ref, v2_ref, q3_ref, k3_ref, v3_ref,
                 za_ref, y_ref, out_ref, lse_ref):
    groups = ((q1_ref, k1_ref, v1_ref), (q2_ref, k2_ref, v2_ref), (q3_ref, k3_ref, v3_ref))
    for gi, (q_ref, k_ref, v_ref) in enumerate(groups):
        r = q_ref.shape[1]
        n_blk = q_ref.shape[3] // SPAN

        def put(blocks, o, lse, gi=gi, r=r):
            for g, (c, n) in enumerate(blocks):
                if r == 1:
                    rows = pl.ds(n * SPAN, SPAN)
                else:
                    pitch = _row_pitch(r)
                    rows = pl.ds(c + pitch * SPAN * n, SPAN, stride=pitch)
                out_ref[gi, rows, :] = o[g]
                lse_ref[gi, rows, :] = jnp.broadcast_to(lse[g], o[g].shape)

        first = [(c, 0) for c in range(r)]
        rest = [(c, n) for c in range(r) for n in range(1, n_blk)]
        for blocks, is_first in ((first, True), (rest, False)):
            for lo in range(0, len(blocks), ATTN_BATCH):
                chunk = blocks[lo:lo + ATTN_BATCH]
                q = jnp.stack([q_ref[0, c, 0, n * SPAN:(n + 1) * SPAN, :] for c, n in chunk])
                k0 = 0 if is_first else 1
                k = jnp.stack([k_ref[0, c, 0, (n - k0) * SPAN:(n + 1) * SPAN, :] for c, n in chunk])
                v = jnp.stack([v_ref[0, c, 0, (n - k0) * SPAN:(n + 1) * SPAN, :] for c, n in chunk])
                o, lse = _attn_blocks(q, k, v, is_first)
                put(chunk, o, lse)

    seq = y_ref.shape[2]

    def in_time_order(ref, gi):
        r = groups[gi][0].shape[1]
        pitch = _row_pitch(r)
        if pitch == r:
            return ref[gi, :seq, :]
        padded = ref[gi, :seq // r * pitch, :].reshape(seq // r, pitch, HEAD_DIM)
        return padded[:, :r, :].reshape(seq, HEAD_DIM)

    lse = jnp.stack([in_time_order(lse_ref, gi) for gi in range(len(groups))])
    out = jnp.stack([in_time_order(out_ref, gi) for gi in range(len(groups))])
    mx = jnp.max(lse, axis=0)
    w = jnp.exp2(lse - mx[None])
    y = jnp.sum(w * out, axis=0) / jnp.sum(w, axis=0)
    za = za_ref[0, 0].astype(F32)
    y_ref[0, 0] = (y * (za * jax.nn.sigmoid(za))).astype(y_ref.dtype)


def _attention(o1, o2, o3, za):
    bsz, _, _, seq, _ = o1.shape

    def specs(arr):
        _, r, _, n, _ = arr.shape
        return [pl.BlockSpec((1, r, 1, n, HEAD_DIM),
                             lambda b, h, part=part: (b, 0, part * HEADS_PER_GROUP + h, 0, 0))
                for part in range(3)]

    per_head = pl.BlockSpec((1, 1, seq, HEAD_DIM), lambda b, h: (b, h, 0, 0))
    padded_rows = max(seq // o.shape[1] * _row_pitch(o.shape[1]) for o in (o1, o2, o3))

    return pl.pallas_call(
        _attn_kernel,
        grid=(bsz, HEADS_PER_GROUP),
        in_specs=specs(o1) + specs(o2) + specs(o3) + [per_head],
        out_specs=per_head,
        out_shape=jax.ShapeDtypeStruct((bsz, HEADS_PER_GROUP, seq, HEAD_DIM), BF16),
        scratch_shapes=[
            pltpu.VMEM((3, padded_rows, HEAD_DIM), F32),
            pltpu.VMEM((3, padded_rows, HEAD_DIM), F32),
        ],
        compiler_params=pltpu.CompilerParams(
            dimension_semantics=("arbitrary", "arbitrary"), vmem_limit_bytes=VMEM_LIMIT),
        name="dilated_attention",
    )(o1, o1, o1, o2, o2, o2, o3, o3, o3, za)


def _merge_kernel(x_ref, ys_ref, ya_ref, gs_ref, ga_ref, wbs_ref, wba_ref, wout_ref, pg_ref, o_ref):
    ms = jnp.dot(ys_ref[0], wbs_ref[0], preferred_element_type=F32)
    ya = jnp.concatenate([ya_ref[0, hh] for hh in range(HEADS_PER_GROUP)], axis=1)
    ma = jnp.dot(ya, wba_ref[0], preferred_element_type=F32)
    merged = (jax.nn.sigmoid(gs_ref[0].astype(F32)) * ms
              + jax.nn.sigmoid(ga_ref[0].astype(F32)) * ma)
    out = jnp.dot(merged.astype(BF16), wout_ref[0], preferred_element_type=F32)
    o_ref[0] = x_ref[0] + _rmsnorm_f32(out, pg_ref[0])


def _merge(x, ys, ya, gates, wbs_all, wba_all, wout_all, pg_all, layer, *, tm=1024):
    bsz, seq, d_model = x.shape
    row = lambda width, blk: pl.BlockSpec((1, tm, width), lambda b, i: (b, i, blk))
    per_layer = lambda arr: pl.BlockSpec((1,) + arr.shape[1:], lambda b, i: (layer, 0, 0))
    return pl.pallas_call(
        _merge_kernel,
        grid=(bsz, seq // tm),
        in_specs=[
            row(d_model, 0), row(SSM_WIDTH, 0),
            pl.BlockSpec((1, HEADS_PER_GROUP, tm, HEAD_DIM), lambda b, i: (b, 0, i, 0)),
            row(d_model, 0),
            row(d_model, 1),
            per_layer(wbs_all), per_layer(wba_all), per_layer(wout_all), per_layer(pg_all),
        ],
        out_specs=row(d_model, 0),
        out_shape=jax.ShapeDtypeStruct((bsz, seq, d_model), F32),
        compiler_params=pltpu.CompilerParams(
            dimension_semantics=("arbitrary", "arbitrary"), vmem_limit_bytes=VMEM_LIMIT),
        name="merge_out",
    )(x, ys, ya, gates, gates, wbs_all, wba_all, wout_all, pg_all)


def _ssm_params(lam_re, lam_im, log_dt, b_re, b_im, c_re, c_im):
    depth = lam_re.shape[0]
    a = jnp.minimum(lam_re, -1e-4)
    b = lam_im
    dt = jnp.exp(log_dt)[..., None]
    mag = jnp.exp(a * dt)
    lbr = mag * jnp.cos(b * dt)
    lbi = mag * jnp.sin(b * dt)
    nr, ni = lbr - 1.0, lbi
    den = a * a + b * b
    kr = (nr * a + ni * b) / den
    ki = (ni * a - nr * b) / den
    bbr = kr[..., None] * b_re - ki[..., None] * b_im
    bbi = kr[..., None] * b_im + ki[..., None] * b_re
    per_block = LANES // (2 * SSM_GROUP)
    eye2 = jnp.eye(2, dtype=F32)
    slot = jax.nn.one_hot(jnp.arange(N_STATE_TILES) % per_block, per_block, dtype=F32)
    b5 = jnp.stack([bbr, bbi], axis=1).reshape(depth, 2, N_STATE_TILES, 2, SSM_STATE, SSM_GROUP)
    bblk = jnp.einsum('langpc,gh->lngcahp', b5, eye2).reshape(
        depth, N_STATE_TILES, 2 * SSM_GROUP, STATE_TILE)
    bmat = (bblk[:, :, None] * slot[None, :, :, None, None]).reshape(
        depth, N_STATE_TILES, LANES, STATE_TILE).astype(BF16)
    c5 = jnp.stack([c_re, -c_im], axis=1).reshape(depth, 2, N_STATE_TILES, 2, SSM_GROUP, SSM_STATE)
    cblk = jnp.einsum('langcp,gh->lnagphc', c5, eye2).reshape(
        depth, N_STATE_TILES, STATE_TILE, 2 * SSM_GROUP)
    cmat = (cblk[:, :, :, None, :] * slot[None, :, None, :, None]).reshape(
        depth, N_STATE_TILES // per_block, per_block * STATE_TILE, LANES).astype(BF16)
    return bmat, cmat, lbr.reshape(depth, 1, N_STATE), lbi.reshape(depth, 1, N_STATE)


def kernel(x, pre_norm_g, w_in, lambda_re, lambda_im, log_dt, b_re, b_im, c_re, c_im, d_skip, w_glu, b_glu, w_branch_s, w_branch_a, w_out, post_norm_g):
    depth = w_in.shape[0]
    assert x.shape[-1] == D_MODEL and w_in.shape[-1] == COL_GA + D_MODEL
    w_in_bf = w_in.astype(BF16)
    w_glu_bf = w_glu.astype(BF16)
    wbs_bf, wba_bf, wout_bf = (w.astype(BF16) for w in (w_branch_s, w_branch_a, w_out))
    pre_g = pre_norm_g[:, None, :]
    post_g = post_norm_g[:, None, :]
    bmat, cmat, lre, lim = _ssm_params(lambda_re, lambda_im, log_dt, b_re, b_im, c_re, c_im)
    for l in range(depth):
        o1, o2, o3, uz, gates, za = _in_projection(x, pre_g, w_in_bf, l)
        ys = _s5_branch(uz, bmat[l], cmat[l], lre[l], lim[l], d_skip[l][None, :], w_glu_bf,
                        b_glu[l][None, :], l)
        ya = _attention(o1, o2, o3, za)
        x = _merge(x, ys, ya, gates, wbs_bf, wba_bf, wout_bf, post_g, l)
    return x
```

```python
import functools

import jax
import jax.numpy as jnp
from jax import lax
from jax.experimental import pallas as pl
from jax.experimental.pallas import tpu as pltpu

F32 = jnp.float32
BF16 = jnp.bfloat16

RMS_EPS = 1e-6
LOG2E = 1.4426950408889634
SSM_WIDTH = 512
SSM_GROUP = 16
SSM_GROUPS = 32
SSM_STATE = 64
N_STATE = SSM_GROUPS * SSM_STATE
HEAD_DIM = 128
HEADS_PER_GROUP = 4
SPAN = 128
ATTN_WIDTH = HEADS_PER_GROUP * HEAD_DIM
LANES = 128
VMEM_LIMIT = 56 * 1024 * 1024

PROJ_TN = 512
D_MODEL = 1024
COL_U, COL_ZS = 0, SSM_WIDTH
COL_Q = 2 * SSM_WIDTH
COL_K = COL_Q + 3 * ATTN_WIDTH
COL_V = COL_K + 3 * ATTN_WIDTH
COL_ZA = COL_V + 3 * ATTN_WIDTH
COL_GS = COL_ZA + ATTN_WIDTH
COL_GA = COL_GS + D_MODEL
UZ_SRC = (COL_U, COL_ZS)
GATE_SRC = (COL_GS, COL_GS + PROJ_TN, COL_GA, COL_GA + PROJ_TN)
GATE_WIDTH = len(GATE_SRC) * PROJ_TN
QKV_SLOTS = 3 * HEADS_PER_GROUP
ATTN_BATCH = 16
STATE_TILE = 2 * LANES
N_STATE_TILES = 2 * N_STATE // STATE_TILE


def _rmsnorm_f32(x, g):
    inv = lax.rsqrt(jnp.mean(x * x, axis=-1, keepdims=True) + RMS_EPS)
    return x * inv * g


def _proj_kernel(x_ref, g_ref, w_ref, o1_ref, o2_ref, o3_ref, uz_ref, gate_ref, za_ref,
                 hf_ref, hn_ref, h4_ref, h16_ref, *, tm):
    d_model = x_ref.shape[-1]
    n_slab = d_model // LANES
    hf = _rmsnorm_f32(x_ref[0], g_ref[0])
    hn_ref[...] = hf.astype(BF16)
    for k in range(n_slab):
        hf_ref[k] = hf[:, k * LANES:(k + 1) * LANES]
    for r, dst in ((4, h4_ref), (16, h16_ref)):
        rows = tm // r
        for c in range(r):
            for k in range(n_slab):
                dst[c * rows:(c + 1) * rows, k * LANES:(k + 1) * LANES] = (
                    hf_ref[k, pl.ds(c, rows, stride=r), :].astype(BF16))

    def tile(h_ref, col):
        return jnp.dot(h_ref[...], w_ref[0, :, col:col + PROJ_TN], preferred_element_type=F32)

    def heads(t):
        return [t[:, hh * HEAD_DIM:(hh + 1) * HEAD_DIM] for hh in range(HEADS_PER_GROUP)]

    for j, base in enumerate((COL_Q, COL_K, COL_V)):
        for gi, (o_ref, h_ref, r) in enumerate(((o1_ref, hn_ref, 1), (o2_ref, h4_ref, 4),
                                               (o3_ref, h16_ref, 16))):
            t = tile(h_ref, base + gi * ATTN_WIDTH).astype(BF16)
            for hh, th in enumerate(heads(t)):
                o_ref[0, :, j * HEADS_PER_GROUP + hh, :, :] = th.reshape(r, tm // r, HEAD_DIM)
    for hh, th in enumerate(heads(tile(hn_ref, COL_ZA).astype(BF16))):
        za_ref[0, hh] = th
    for jj, col in enumerate(UZ_SRC):
        uz_ref[0, :, jj * PROJ_TN:(jj + 1) * PROJ_TN] = tile(hn_ref, col)
    for jj, col in enumerate(GATE_SRC):
        gate_ref[0, :, jj * PROJ_TN:(jj + 1) * PROJ_TN] = tile(hn_ref, col).astype(gate_ref.dtype)


def _in_projection(x, g_all, w_all, layer, *, tm=512):
    bsz, seq, d_model = x.shape
    n_i = seq // tm
    kern = functools.partial(_proj_kernel, tm=tm)
    qkv = lambda r: (1, r, QKV_SLOTS, tm // r, HEAD_DIM)
    qkv_all = lambda r: (bsz, r, QKV_SLOTS, seq // r, HEAD_DIM)
    return pl.pallas_call(
        kern,
        grid=(bsz, n_i),
        in_specs=[
            pl.BlockSpec((1, tm, d_model), lambda b, i: (b, i, 0)),
            pl.BlockSpec((1, 1, d_model), lambda b, i: (layer, 0, 0)),
            pl.BlockSpec((1,) + w_all.shape[1:], lambda b, i: (layer, 0, 0),
                         pipeline_mode=pl.Buffered(1)),
        ],
        out_specs=[
            pl.BlockSpec(qkv(1), lambda b, i: (b, 0, 0, i, 0)),
            pl.BlockSpec(qkv(4), lambda b, i: (b, 0, 0, i, 0)),
            pl.BlockSpec(qkv(16), lambda b, i: (b, 0, 0, i, 0)),
            pl.BlockSpec((1, tm, 2 * SSM_WIDTH), lambda b, i: (b, i, 0)),
            pl.BlockSpec((1, tm, GATE_WIDTH), lambda b, i: (b, i, 0)),
            pl.BlockSpec((1, HEADS_PER_GROUP, tm, HEAD_DIM), lambda b, i: (b, 0, i, 0)),
        ],
        out_shape=[
            jax.ShapeDtypeStruct(qkv_all(1), BF16),
            jax.ShapeDtypeStruct(qkv_all(4), BF16),
            jax.ShapeDtypeStruct(qkv_all(16), BF16),
            jax.ShapeDtypeStruct((bsz, seq, 2 * SSM_WIDTH), F32),
            jax.ShapeDtypeStruct((bsz, seq, GATE_WIDTH), BF16),
            jax.ShapeDtypeStruct((bsz, HEADS_PER_GROUP, seq, HEAD_DIM), BF16),
        ],
        scratch_shapes=[
            pltpu.VMEM((d_model // LANES, tm, LANES), F32),
            pltpu.VMEM((tm, d_model), BF16),
            pltpu.VMEM((tm, d_model), BF16),
            pltpu.VMEM((tm, d_model), BF16),
        ],
        compiler_params=pltpu.CompilerParams(
            dimension_semantics=("arbitrary", "arbitrary"),
            vmem_limit_bytes=VMEM_LIMIT),
        name="in_projection",
    )(x, g_all, w_all)


def _s5_kernel(uz_ref, bmat_ref, cmat_ref, lre_ref, lim_ref, dskip_ref, wglu_ref, bglu_ref,
               ys_ref, drive_ref, h_ref, state_ref, *, tt, bsz, tiles_per_chunk, scan_unroll):
    @pl.when(pl.program_id(0) == 0)
    def _():
        state_ref[...] = jnp.zeros_like(state_ref)

    rows = tt * bsz
    u = jnp.swapaxes(uz_ref[:, :, :SSM_WIDTH], 0, 1).reshape(rows, SSM_WIDTH)
    u_bf = u.astype(BF16)
    for n in range(N_STATE_TILES):
        blk = n // tiles_per_chunk
        drive_ref[:, n * STATE_TILE:(n + 1) * STATE_TILE] = jnp.dot(
            u_bf[:, blk * LANES:(blk + 1) * LANES], bmat_ref[n], preferred_element_type=F32)

    for n0 in range(0, N_STATE_TILES, tiles_per_chunk):
        tiles = range(n0, n0 + tiles_per_chunk)
        lam = [(jnp.broadcast_to(lre_ref[:, n * LANES:(n + 1) * LANES], (bsz, LANES)),
                jnp.broadcast_to(lim_ref[:, n * LANES:(n + 1) * LANES], (bsz, LANES))) for n in tiles]

        def step(t, carry, tiles=tiles, lam=lam):
            rows_t = pl.ds(pl.multiple_of(t * bsz, bsz), bsz)
            out = []
            for (hre, him), (lre, lim), n in zip(carry, lam, tiles):
                re_sl = slice(n * STATE_TILE, n * STATE_TILE + LANES)
                im_sl = slice(n * STATE_TILE + LANES, (n + 1) * STATE_TILE)
                nre = lre * hre - lim * him + drive_ref[rows_t, re_sl]
                nim = lre * him + lim * hre + drive_ref[rows_t, im_sl]
                h_ref[rows_t, re_sl] = nre
                h_ref[rows_t, im_sl] = nim
                out.append((nre, nim))
            return tuple(out)

        init = tuple((state_ref[:, n * STATE_TILE:n * STATE_TILE + LANES],
                      state_ref[:, n * STATE_TILE + LANES:(n + 1) * STATE_TILE]) for n in tiles)
        final = lax.fori_loop(0, tt, step, init, unroll=scan_unroll)
        for (hre, him), n in zip(final, tiles):
            state_ref[:, n * STATE_TILE:n * STATE_TILE + LANES] = hre
            state_ref[:, n * STATE_TILE + LANES:(n + 1) * STATE_TILE] = him

    chunk = tiles_per_chunk * STATE_TILE
    y = jnp.concatenate(
        [jnp.dot(h_ref[:, m * chunk:(m + 1) * chunk].astype(BF16), cmat_ref[m],
                 preferred_element_type=F32) for m in range(SSM_WIDTH // LANES)], axis=1)
    y = y + dskip_ref[...] * u
    y = jax.nn.gelu(y)
    gate = jnp.dot(y.astype(BF16), wglu_ref[0], preferred_element_type=F32) + bglu_ref[...]
    y = y * jax.nn.sigmoid(gate)
    z = uz_ref[:, :, SSM_WIDTH:]
    ys = jnp.swapaxes(y.reshape(tt, bsz, SSM_WIDTH), 0, 1) * (z * jax.nn.sigmoid(z))
    ys_ref[...] = ys.astype(ys_ref.dtype)


def _s5_branch(uz, bmat, cmat, lre, lim, d_skip, w_glu_all, b_glu, layer, *, tt=128, scan_unroll=128):
    bsz, seq, _ = uz.shape
    tiles_per_chunk = LANES // (2 * SSM_GROUP)
    kern = functools.partial(_s5_kernel, tt=tt, bsz=bsz, tiles_per_chunk=tiles_per_chunk,
                             scan_unroll=scan_unroll)
    full = lambda shape: pl.BlockSpec(shape, lambda i: (0,) * len(shape))
    return pl.pallas_call(
        kern,
        grid=(seq // tt,),
        in_specs=[
            pl.BlockSpec((bsz, tt, 2 * SSM_WIDTH), lambda i: (0, i, 0)),
            full(bmat.shape), full(cmat.shape), full(lre.shape), full(lim.shape),
            full(d_skip.shape),
            pl.BlockSpec((1,) + w_glu_all.shape[1:], lambda i: (layer, 0, 0)),
            full(b_glu.shape),
        ],
        out_specs=pl.BlockSpec((bsz, tt, SSM_WIDTH), lambda i: (0, i, 0)),
        out_shape=jax.ShapeDtypeStruct((bsz, seq, SSM_WIDTH), BF16),
        scratch_shapes=[
            pltpu.VMEM((tt * bsz, 2 * N_STATE), F32),
            pltpu.VMEM((tt * bsz, 2 * N_STATE), F32),
            pltpu.VMEM((bsz, 2 * N_STATE), F32),
        ],
        compiler_params=pltpu.CompilerParams(
            dimension_semantics=("arbitrary",), vmem_limit_bytes=VMEM_LIMIT),
        name="s5_branch",
    )(uz, bmat, cmat, lre, lim, d_skip, w_glu_all, b_glu)


def _attn_blocks(q, k, v, first):
    s = jnp.einsum('gqd,gkd->gqk', q, k, preferred_element_type=F32) * (HEAD_DIM ** -0.5 * LOG2E)
    qi = lax.broadcasted_iota(jnp.int32, s.shape[1:], 0)
    ki = lax.broadcasted_iota(jnp.int32, s.shape[1:], 1)
    valid = (ki <= qi) if first else ((ki >= qi) & (ki <= qi + SPAN))
    s = jnp.where(valid[None], s, -jnp.inf)
    m = jnp.max(s, axis=-1, keepdims=True)
    p = jnp.exp2(s - m).astype(BF16)
    v_ones = jnp.concatenate([v, jnp.ones_like(v)], axis=-1)
    acc = jnp.einsum('gqk,gkd->gqd', p, v_ones, preferred_element_type=F32)
    l = acc[..., HEAD_DIM:]
    return acc[..., :HEAD_DIM] / l, m + jnp.log2(l)


def _row_pitch(r):
    return r if r % 16 else r + r // 2


def _attn_kernel(q1_ref, k1_ref, v1_ref, q2_ref, k2_ref, v2_ref, q3_ref, k3_ref, v3_ref,
                 za_ref, y_ref, out_ref, lse_ref):
    groups = ((q1_ref, k1_ref, v1_ref), (q2_ref, k2_ref, v2_ref), (q3_ref, k3_ref, v3_ref))
    for gi, (q_ref, k_ref, v_ref) in enumerate(groups):
        r = q_ref.shape[1]
        n_blk = q_ref.shape[3] // SPAN

        def put(blocks, o, lse, gi=gi, r=r):
            for g, (c, n) in enumerate(blocks):
                if r == 1:
                    rows = pl.ds(n * SPAN, SPAN)
                else:
                    pitch = _row_pitch(r)
                    rows = pl.ds(c + pitch * SPAN * n, SPAN, stride=pitch)
                out_ref[gi, rows, :] = o[g]
                lse_ref[gi, rows, :] = lse[g]

        first = [(c, 0) for c in range(r)]
        rest = [(c, n) for c in range(r) for n in range(1, n_blk)]
        for blocks, is_first in ((first, True), (rest, False)):
            for lo in range(0, len(blocks), ATTN_BATCH):
                chunk = blocks[lo:lo + ATTN_BATCH]
                q = jnp.stack([q_ref[0, c, 0, n * SPAN:(n + 1) * SPAN, :] for c, n in chunk])
                k0 = 0 if is_first else 1
                k = jnp.stack([k_ref[0, c, 0, (n - k0) * SPAN:(n + 1) * SPAN, :] for c, n in chunk])
                v = jnp.stack([v_ref[0, c, 0, (n - k0) * SPAN:(n + 1) * SPAN, :] for c, n in chunk])
                o, lse = _attn_blocks(q, k, v, is_first)
                put(chunk, o, lse)

    seq = y_ref.shape[2]

    def in_time_order(ref, gi):
        r = groups[gi][0].shape[1]
        pitch = _row_pitch(r)
        if pitch == r:
            return ref[gi, :seq, :]
        padded = ref[gi, :seq // r * pitch, :].reshape(seq // r, pitch, HEAD_DIM)
        return padded[:, :r, :].reshape(seq, HEAD_DIM)

    lse = jnp.stack([in_time_order(lse_ref, gi) for gi in range(len(groups))])
    out = jnp.stack([in_time_order(out_ref, gi) for gi in range(len(groups))])
    mx = jnp.max(lse, axis=0)
    w = jnp.exp2(lse - mx[None])
    y = jnp.sum(w * out, axis=0) / jnp.sum(w, axis=0)
    za = za_ref[0, 0].astype(F32)
    y_ref[0, 0] = (y * (za * jax.nn.sigmoid(za))).astype(y_ref.dtype)


def _attention(o1, o2, o3, za):
    bsz, _, _, seq, _ = o1.shape

    def specs(arr):
        _, r, _, n, _ = arr.shape
        return [pl.BlockSpec((1, r, 1, n, HEAD_DIM),
                             lambda b, h, part=part: (b, 0, part * HEADS_PER_GROUP + h, 0, 0))
                for part in range(3)]

    per_head = pl.BlockSpec((1, 1, seq, HEAD_DIM), lambda b, h: (b, h, 0, 0))
    padded_rows = max(seq // o.shape[1] * _row_pitch(o.shape[1]) for o in (o1, o2, o3))

    return pl.pallas_call(
        _attn_kernel,
        grid=(bsz, HEADS_PER_GROUP),
        in_specs=specs(o1) + specs(o2) + specs(o3) + [per_head],
        out_specs=per_head,
        out_shape=jax.ShapeDtypeStruct((bsz, HEADS_PER_GROUP, seq, HEAD_DIM), BF16),
        scratch_shapes=[
            pltpu.VMEM((3, padded_rows, HEAD_DIM), F32),
            pltpu.VMEM((3, padded_rows, HEAD_DIM), F32),
        ],
        compiler_params=pltpu.CompilerParams(
            dimension_semantics=("arbitrary", "arbitrary"), vmem_limit_bytes=VMEM_LIMIT),
        name="dilated_attention",
    )(o1, o1, o1, o2, o2, o2, o3, o3, o3, za)


def _merge_kernel(x_ref, ys_ref, ya_ref, gs_ref, ga_ref, wbs_ref, wba_ref, wout_ref, pg_ref, o_ref):
    ms = jnp.dot(ys_ref[0], wbs_ref[0], preferred_element_type=F32)
    ya = jnp.concatenate([ya_ref[0, hh] for hh in range(HEADS_PER_GROUP)], axis=1)
    ma = jnp.dot(ya, wba_ref[0], preferred_element_type=F32)
    merged = (jax.nn.sigmoid(gs_ref[0].astype(F32)) * ms
              + jax.nn.sigmoid(ga_ref[0].astype(F32)) * ma)
    out = jnp.dot(merged.astype(BF16), wout_ref[0], preferred_element_type=F32)
    o_ref[0] = x_ref[0] + _rmsnorm_f32(out, pg_ref[0])


def _merge(x, ys, ya, gates, wbs_all, wba_all, wout_all, pg_all, layer, *, tm=1024):
    bsz, seq, d_model = x.shape
    row = lambda width, blk: pl.BlockSpec((1, tm, width), lambda b, i: (b, i, blk))
    per_layer = lambda arr: pl.BlockSpec((1,) + arr.shape[1:], lambda b, i: (layer, 0, 0))
    return pl.pallas_call(
        _merge_kernel,
        grid=(bsz, seq // tm),
        in_specs=[
            row(d_model, 0), row(SSM_WIDTH, 0),
            pl.BlockSpec((1, HEADS_PER_GROUP, tm, HEAD_DIM), lambda b, i: (b, 0, i, 0)),
            row(d_model, 0),
            row(d_model, 1),
            per_layer(wbs_all), per_layer(wba_all), per_layer(wout_all), per_layer(pg_all),
        ],
        out_specs=row(d_model, 0),
        out_shape=jax.ShapeDtypeStruct((bsz, seq, d_model), F32),
        compiler_params=pltpu.CompilerParams(
            dimension_semantics=("arbitrary", "arbitrary"), vmem_limit_bytes=VMEM_LIMIT),
        name="merge_out",
    )(x, ys, ya, gates, gates, wbs_all, wba_all, wout_all, pg_all)


def _ssm_params(lam_re, lam_im, log_dt, b_re, b_im, c_re, c_im):
    depth = lam_re.shape[0]
    a = jnp.minimum(lam_re, -1e-4)
    b = lam_im
    dt = jnp.exp(log_dt)[..., None]
    mag = jnp.exp(a * dt)
    lbr = mag * jnp.cos(b * dt)
    lbi = mag * jnp.sin(b * dt)
    nr, ni = lbr - 1.0, lbi
    den = a * a + b * b
    kr = (nr * a + ni * b) / den
    ki = (ni * a - nr * b) / den
    bbr = kr[..., None] * b_re - ki[..., None] * b_im
    bbi = kr[..., None] * b_im + ki[..., None] * b_re
    per_block = LANES // (2 * SSM_GROUP)
    eye2 = jnp.eye(2, dtype=F32)
    slot = jax.nn.one_hot(jnp.arange(N_STATE_TILES) % per_block, per_block, dtype=F32)
    b5 = jnp.stack([bbr, bbi], axis=1).reshape(depth, 2, N_STATE_TILES, 2, SSM_STATE, SSM_GROUP)
    bblk = jnp.einsum('langpc,gh->lngcahp', b5, eye2).reshape(
        depth, N_STATE_TILES, 2 * SSM_GROUP, STATE_TILE)
    bmat = (bblk[:, :, None] * slot[None, :, :, None, None]).reshape(
        depth, N_STATE_TILES, LANES, STATE_TILE).astype(BF16)
    c5 = jnp.stack([c_re, -c_im], axis=1).reshape(depth, 2, N_STATE_TILES, 2, SSM_GROUP, SSM_STATE)
    cblk = jnp.einsum('langcp,gh->lnagphc', c5, eye2).reshape(
        depth, N_STATE_TILES, STATE_TILE, 2 * SSM_GROUP)
    cmat = (cblk[:, :, :, None, :] * slot[None, :, None, :, None]).reshape(
        depth, N_STATE_TILES // per_block, per_block * STATE_TILE, LANES).astype(BF16)
    return bmat, cmat, lbr.reshape(depth, 1, N_STATE), lbi.reshape(depth, 1, N_STATE)


def kernel(x, pre_norm_g, w_in, lambda_re, lambda_im, log_dt, b_re, b_im, c_re, c_im, d_skip, w_glu, b_glu, w_branch_s, w_branch_a, w_out, post_norm_g):
    depth = w_in.shape[0]
    assert x.shape[-1] == D_MODEL and w_in.shape[-1] == COL_GA + D_MODEL
    w_in_bf = w_in.astype(BF16)
    w_glu_bf = w_glu.astype(BF16)
    wbs_bf, wba_bf, wout_bf = (w.astype(BF16) for w in (w_branch_s, w_branch_a, w_out))
    pre_g = pre_norm_g[:, None, :]
    post_g = post_norm_g[:, None, :]
    bmat, cmat, lre, lim = _ssm_params(lambda_re, lambda_im, log_dt, b_re, b_im, c_re, c_im)
    for l in range(depth):
        o1, o2, o3, uz, gates, za = _in_projection(x, pre_g, w_in_bf, l)
        ys = _s5_branch(uz, bmat[l], cmat[l], lre[l], lim[l], d_skip[l][None, :], w_glu_bf,
                        b_glu[l][None, :], l)
        ya = _attention(o1, o2, o3, za)
        x = _merge(x, ys, ya, gates, wbs_bf, wba_bf, wout_bf, post_g, l)
    return x
```
